```python
import math
import jax, jax.numpy as jnp
from jax import lax
import numpy as np

D_MODEL = 1024
BATCH = 4
SEQ = 8192
DEPTH = 2
DEC_BATCH = 4
DEC_SEQ = 4096
PAST_LEN = 128

GRID_W = 64
MIX_W = D_MODEL
HY_W = MIX_W // 2
ATT_W = MIX_W - HY_W
HEAD_DIM = 64
N_HEADS = ATT_W // HEAD_DIM
WIN_ROWS = 8
WIN_COLS = 16
SHORT_CONV = 3
POS_EMB_DIM = 33
POS_BANDS = (POS_EMB_DIM - 1) // 2
FILTER_HIDDEN = 64
FAST_DECAY_PCT = 0.3
SLOW_DECAY_PCT = 1.5
DECAY_TARGET = 1e-2
EPS = 1e-6
N_IN = 4 * HY_W + 4 * ATT_W

kernel_name = "hymba_hyena_natten_encoder"


def rmsnorm(x, g):
    xf = x.astype(jnp.float32)
    y = xf * lax.rsqrt(jnp.mean(xf * xf, axis=-1, keepdims=True) + EPS)
    return (y * g.astype(jnp.float32)).astype(x.dtype)


def short_conv(u, w, b):
    p = jnp.pad(u, ((0, 0), (1, 1), (0, 0)))
    return w[0] * p[:, :-2] + w[1] * p[:, 1:-1] + w[2] * p[:, 2:] + b


def hyena_filter(L, w1, b1, fr1, w2, b2, fr2, w3):
    f32 = jnp.float32
    t01 = jnp.linspace(0.0, 1.0, L, dtype=f32)[:, None]
    w = 2.0 * math.pi * jnp.arange(L, dtype=f32)[:, None] / L
    f = jnp.linspace(1e-4, POS_BANDS - 1, POS_BANDS, dtype=f32)[None, :]
    z = jnp.concatenate([t01, jnp.cos(f * w), -jnp.sin(f * w)], axis=-1)
    h = jnp.sin(fr1.astype(f32) * (z @ w1.astype(f32) + b1.astype(f32)))
    h = jnp.sin(fr2.astype(f32) * (h @ w2.astype(f32) + b2.astype(f32)))
    h = h @ w3.astype(f32)
    min_decay = math.log(DECAY_TARGET) / SLOW_DECAY_PCT
    max_decay = math.log(DECAY_TARGET) / FAST_DECAY_PCT
    deltas = jnp.linspace(min_decay, max_decay, 2 * HY_W, dtype=f32)[None, :]
    h = h * jnp.exp(-t01 * jnp.abs(deltas))
    h_fwd, h_bwd = h[:, :HY_W], h[:, HY_W:]
    return jnp.concatenate([h_fwd, jnp.zeros((1, HY_W), f32), h_bwd[:0:-1]], axis=0)


def long_conv(u, kfilt, bias):
    L = u.shape[1]
    uf = u.astype(jnp.float32)
    U = jnp.fft.rfft(uf, n=2 * L, axis=1)
    K = jnp.fft.rfft(kfilt, n=2 * L, axis=0)
    y = jnp.fft.irfft(U * K[None], n=2 * L, axis=1)[:, :L]
    return (y + uf * bias.astype(jnp.float32)).astype(u.dtype)


def neighbourhood_attention(q, k, v, rpb):
    B, L = q.shape[0], q.shape[1]
    rows = L // GRID_W
    kr = min(WIN_ROWS, rows)
    grid = lambda a: a.reshape(B, rows, GRID_W, N_HEADS, HEAD_DIM)
    qg, kg, vg = grid(q), grid(k), grid(v)
    cols = np.arange(GRID_W)
    col_start = np.clip(cols - WIN_COLS // 2, 0, GRID_W - WIN_COLS)
    col_idx = col_start[:, None] + np.arange(WIN_COLS)[None, :]
    col_off = col_idx - cols[:, None] + (WIN_COLS - 1)
    rpb_c = rpb.astype(jnp.float32)[:, :, col_off]
    scale = HEAD_DIM ** -0.5

    def one_row(r):
        r0 = jnp.clip(r - kr // 2, 0, rows - kr)
        q_r = lax.dynamic_index_in_dim(qg, r, axis=1, keepdims=False)
        k_band = lax.dynamic_slice_in_dim(kg, r0, kr, axis=1)
        v_band = lax.dynamic_slice_in_dim(vg, r0, kr, axis=1)
        k_win = k_band[:, :, col_idx]
        v_win = v_band[:, :, col_idx]
        row_off = r0 + jnp.arange(kr) - r + (WIN_ROWS - 1)
        bias = rpb_c[:, row_off].transpose(0, 2, 1, 3)
        s = jnp.einsum('bchd,bkcwhd->bhckw', q_r, k_win).astype(jnp.float32) * scale + bias
        p = jax.nn.softmax(s.reshape(B, N_HEADS, GRID_W, kr * WIN_COLS), axis=-1)
        p = p.reshape(B, N_HEADS, GRID_W, kr, WIN_COLS).astype(v.dtype)
        return jnp.einsum('bhckw,bkcwhd->bchd', p, v_win)

    out = lax.map(one_row, jnp.arange(rows))
    return out.transpose(1, 0, 2, 3, 4).reshape(B, L, N_HEADS * HEAD_DIM)


def encoder_layer(x, norm_g, w_in, conv_w, conv_b, f_w1, f_b1, f_fr1, f_w2, f_b2, f_fr2,
                  f_w3, hy_bias, qn_g, kn_g, rpb, on_hy, on_att, w_out):
    B, L, _ = x.shape
    h = rmsnorm(x, norm_g)
    z = h @ w_in
    hy_in, hy_gate, att_qkv, att_gate = jnp.split(
        z, [3 * HY_W, 4 * HY_W, 4 * HY_W + 3 * ATT_W], axis=-1)
    hy_in = short_conv(hy_in, conv_w, conv_b)
    x0, x1, vv = jnp.split(hy_in, 3, axis=-1)
    kfilt = hyena_filter(L, f_w1, f_b1, f_fr1, f_w2, f_b2, f_fr2, f_w3)
    y_hy = x0 * long_conv(x1 * vv, kfilt, hy_bias)
    y_hy = rmsnorm(y_hy, on_hy) * jax.nn.silu(hy_gate)
    q, k, v = jnp.split(att_qkv, 3, axis=-1)
    q = rmsnorm(q.reshape(B, L, N_HEADS, HEAD_DIM), qn_g)
    k = rmsnorm(k.reshape(B, L, N_HEADS, HEAD_DIM), kn_g)
    v = v.reshape(B, L, N_HEADS, HEAD_DIM)
    y_att = neighbourhood_attention(q, k, v, rpb)
    y_att = rmsnorm(y_att, on_att) * jax.nn.silu(att_gate)
    return x + jnp.concatenate([y_hy, y_att], axis=-1) @ w_out


def trunk(x, norm_g, w_in, conv_w, conv_b, f_w1, f_b1, f_fr1, f_w2, f_b2, f_fr2, f_w3,
          hy_bias, qn_g, kn_g, rpb, on_hy, on_att, w_out):
    for i in range(DEPTH):
        x = encoder_layer(x, norm_g[i], w_in[i], conv_w[i], conv_b[i], f_w1[i], f_b1[i],
                          f_fr1[i], f_w2[i], f_b2[i], f_fr2[i], f_w3[i], hy_bias[i],
                          qn_g[i], kn_g[i], rpb[i], on_hy[i], on_att[i], w_out[i])
    return x


def setup_inputs(seed: int = 0) -> dict:
    key = jax.random.key(seed)
    ks = jax.random.split(key, 24)
    f32 = jnp.float32
    nrm = lambda k, shape, s: jax.random.normal(k, shape, f32) * s
    gain = lambda k, shape: 1.0 + 0.01 * jax.random.normal(k, shape, f32)
    return {
        "x_prompt": jax.random.normal(ks[0], (BATCH, SEQ, D_MODEL), f32),
        "x_sample": jax.random.normal(ks[1], (DEC_BATCH, DEC_SEQ, D_MODEL), f32),
        "norm_g": gain(ks[2], (DEPTH, D_MODEL)),
        "w_in": nrm(ks[3], (DEPTH, D_MODEL, N_IN), D_MODEL ** -0.5),
        "conv_w": nrm(ks[4], (DEPTH, SHORT_CONV, 3 * HY_W), SHORT_CONV ** -0.5),
        "conv_b": nrm(ks[5], (DEPTH, 3 * HY_W), 0.01),
        "f_w1": nrm(ks[6], (DEPTH, POS_EMB_DIM, FILTER_HIDDEN), POS_EMB_DIM ** -0.5),
        "f_b1": nrm(ks[7], (DEPTH, FILTER_HIDDEN), 0.1),
        "f_fr1": gain(ks[8], (DEPTH, FILTER_HIDDEN)),
        "f_w2": nrm(ks[9], (DEPTH, FILTER_HIDDEN, FILTER_HIDDEN), FILTER_HIDDEN ** -0.5),
        "f_b2": nrm(ks[10], (DEPTH, FILTER_HIDDEN), 0.1),
        "f_fr2": gain(ks[11], (DEPTH, FILTER_HIDDEN)),
        "f_w3": nrm(ks[12], (DEPTH, FILTER_HIDDEN, 2 * HY_W), FILTER_HIDDEN ** -0.5),
        "hy_bias": nrm(ks[13], (DEPTH, HY_W), 0.5),
        "qn_g": gain(ks[14], (DEPTH, HEAD_DIM)),
        "kn_g": gain(ks[15], (DEPTH, HEAD_DIM)),
        "rpb": nrm(ks[16], (DEPTH, N_HEADS, 2 * WIN_ROWS - 1, 2 * WIN_COLS - 1), 0.02),
        "on_hy": gain(ks[17], (DEPTH, HY_W)),
        "on_att": gain(ks[18], (DEPTH, ATT_W)),
        "w_out": nrm(ks[19], (DEPTH, MIX_W, D_MODEL), MIX_W ** -0.5),
    }


def reference(x_prompt, x_sample, norm_g, w_in, conv_w, conv_b, f_w1, f_b1, f_fr1, f_w2,
              f_b2, f_fr2, f_w3, hy_bias, qn_g, kn_g, rpb, on_hy, on_att, w_out):
    y_prompt = trunk(x_prompt, norm_g, w_in, conv_w, conv_b, f_w1, f_b1, f_fr1, f_w2, f_b2,
                     f_fr2, f_w3, hy_bias, qn_g, kn_g, rpb, on_hy, on_att, w_out)
    y_sample = trunk(x_sample, norm_g, w_in, conv_w, conv_b, f_w1, f_b1, f_fr1, f_w2, f_b2,
                     f_fr2, f_w3, hy_bias, qn_g, kn_g, rpb, on_hy, on_att, w_out)
    return (y_prompt, y_sample)
```

```python
import functools
import math

import numpy as np
import jax
import jax.numpy as jnp
from jax import lax
from jax.experimental import pallas as pl
from jax.experimental.pallas import tpu as pltpu

F32 = jnp.float32
BF16 = jnp.bfloat16

D_MODEL = 1024
GRID_W = 64
HY_W = 512
ATT_W = 512
HEAD_DIM = 64
N_HEADS = 8
WIN_ROWS = 8
WIN_COLS = 16
POS_BANDS = 16
FILTER_HIDDEN = 64
N_IN = 4 * HY_W + 4 * ATT_W
EPS = 1e-6
NEG_INF = -1e30

CB = 512
DFT_N2 = 128
TOKEN_TILE = 512
HALO = 16
ATT_ROWS = 8
MID_KB = 8
COL_TILE = 4096


def _dot(a, b):
    return jnp.dot(a, b, preferred_element_type=F32)


def _dot_hi(a, b):
    return jnp.dot(a, b, preferred_element_type=F32, precision=lax.Precision.HIGHEST)


def _split(x):
    hi = x.astype(BF16)
    lo = (x - hi.astype(F32)).astype(BF16)
    return hi, lo


def _dot3(a, b):
    ah, al = _split(a)
    bh, bl = _split(b)
    return _dot(ah, bh) + (_dot(ah, bl) + _dot(al, bh))


def _inproj_kernel(x_ref, g_ref, w_ref, qg_ref, kg_ref, bd_ref, z_ref):
    x = x_ref[...]
    ms = jnp.mean(x * x, axis=-1, keepdims=True)
    h = (x * lax.rsqrt(ms + EPS) * g_ref[...]).astype(BF16)
    for j in range(N_IN // CB):
        zj = _dot(h, w_ref[:, j * CB:(j + 1) * CB])
        if j in (4, 5):
            hi, lo = _split(zj * zj)
            hm = _dot(hi, bd_ref[...]) + _dot(lo, bd_ref[...])
            gain = qg_ref[...] if j == 4 else kg_ref[...]
            zj = zj * lax.rsqrt(hm + EPS) * gain
        z_ref[:, j * CB:(j + 1) * CB] = zj.astype(z_ref.dtype)


def _inproj(x2, norm_g, w_in_bf, qgain, kgain, bd):
    t = x2.shape[0]
    tm = TOKEN_TILE
    const = lambda i: (0, 0)
    return pl.pallas_call(
        _inproj_kernel,
        out_shape=jax.ShapeDtypeStruct((t, N_IN), BF16),
        grid=(t // tm,),
        in_specs=[
            pl.BlockSpec((tm, D_MODEL), lambda i: (i, 0)),
            pl.BlockSpec((1, D_MODEL), const),
            pl.BlockSpec((D_MODEL, N_IN), const),
            pl.BlockSpec((1, CB), const),
            pl.BlockSpec((1, CB), const),
            pl.BlockSpec((CB, CB), const),
        ],
        out_specs=pl.BlockSpec((tm, N_IN), lambda i: (i, 0)),
        compiler_params=pltpu.CompilerParams(dimension_semantics=("arbitrary",)),
        name="inproj",
    )(x2, norm_g, w_in_bf, qgain, kgain, bd)


def _hypre_kernel(c0, c1, c2, p0, p1, p2, n0, n1, n2, cw_ref, cb_ref, u_ref, x0_ref, *, tl):
    i = pl.program_id(1)
    last = pl.num_programs(1) - 1
    rows = lax.broadcasted_iota(jnp.int32, (tl, CB), 0)

    def conv(cur_ref, prev_ref, next_ref, c):
        cur = cur_ref[...].astype(F32)
        prev_row = jnp.where(i > 0, prev_ref[HALO - 1:HALO, :].astype(F32), 0.0)
        next_row = jnp.where(i < last, next_ref[0:1, :].astype(F32), 0.0)
        dn = jnp.where(rows == 0, prev_row, pltpu.roll(cur, 1, 0))
        up = jnp.where(rows == tl - 1, next_row, pltpu.roll(cur, tl - 1, 0))
        w = cw_ref[:, c * CB:(c + 1) * CB]
        b = cb_ref[:, c * CB:(c + 1) * CB]
        return w[0:1, :] * dn + w[1:2, :] * cur + w[2:3, :] * up + b

    x0_ref[...] = conv(c0, p0, n0, 0).astype(x0_ref.dtype)
    u_ref[...] = (conv(c1, p1, n1, 1) * conv(c2, p2, n2, 2)).astype(u_ref.dtype)


def _hypre(z3, conv_w, conv_b):
    b, l, _ = z3.shape
    tl = TOKEN_TILE
    hb = tl // HALO
    nh = l // HALO
    cur = lambda c: pl.BlockSpec((None, tl, CB), lambda bi, i, c=c: (bi, i, c))
    prv = lambda c: pl.BlockSpec((None, HALO, CB), lambda bi, i, c=c: (bi, jnp.maximum(i * hb - 1, 0), c))
    nxt = lambda c: pl.BlockSpec((None, HALO, CB), lambda bi, i, c=c: (bi, jnp.minimum((i + 1) * hb, nh - 1), c))
    const = lambda bi, i: (0, 0)
    out = pl.BlockSpec((None, tl, CB), lambda bi, i: (bi, i, 0))
    return pl.pallas_call(
        functools.partial(_hypre_kernel, tl=tl),
        out_shape=(jax.ShapeDtypeStruct((b, l, CB), BF16), jax.ShapeDtypeStruct((b, l, CB), BF16)),
        grid=(b, l // tl),
        in_specs=[cur(0), cur(1), cur(2), prv(0), prv(1), prv(2), nxt(0), nxt(1), nxt(2),
                  pl.BlockSpec((3, 3 * HY_W), const), pl.BlockSpec((1, 3 * HY_W), const)],
        out_specs=(out, out),
        compiler_params=pltpu.CompilerParams(dimension_semantics=("arbitrary", "arbitrary")),
        name="hyena_pre",
    )(z3, z3, z3, z3, z3, z3, z3, z3, z3, conv_w, conv_b)


def _filter_kernel(z_ref, w1_ref, b1_ref, fr1_ref, w2_ref, b2_ref, fr2_ref, w3_ref, dl_ref, o_ref, *, l, tn):
    i = pl.program_id(0)
    z = z_ref[...]
    h = jnp.sin(fr1_ref[...] * (_dot_hi(z, w1_ref[...]) + b1_ref[...]))
    h = jnp.sin(fr2_ref[...] * (_dot_hi(h, w2_ref[...]) + b2_ref[...]))
    h = _dot_hi(h, w3_ref[...])
    h = h * jnp.exp(-z[:, 0:1] * dl_ref[...])
    row = i * tn + lax.broadcasted_iota(jnp.int32, h.shape, 0)
    o_ref[...] = jnp.where(row == l, 0.0, h)


def _filter(zpos2, w1p, b1, fr1, w2, b2, fr2, w3, absdeltas, l):
    n = 2 * l
    tn = min(512, l)
    nb_half = l // tn
    const = lambda i: (0, 0)
    half = lambda i: (0, i // nb_half)
    return pl.pallas_call(
        functools.partial(_filter_kernel, l=l, tn=tn),
        out_shape=jax.ShapeDtypeStruct((n, HY_W), F32),
        grid=(n // tn,),
        in_specs=[
            pl.BlockSpec((tn, 128), lambda i: (i, 0)),
            pl.BlockSpec((128, FILTER_HIDDEN), const),
            pl.BlockSpec((1, FILTER_HIDDEN), const),
            pl.BlockSpec((1, FILTER_HIDDEN), const),
            pl.BlockSpec((FILTER_HIDDEN, FILTER_HIDDEN), const),
            pl.BlockSpec((1, FILTER_HIDDEN), const),
            pl.BlockSpec((1, FILTER_HIDDEN), const),
            pl.BlockSpec((FILTER_HIDDEN, HY_W), half),
            pl.BlockSpec((1, HY_W), half),
        ],
        out_specs=pl.BlockSpec((tn, HY_W), lambda i: (i, 0)),
        compiler_params=pltpu.CompilerParams(dimension_semantics=("arbitrary",)),
        name="hyena_filter",
    )(zpos2, w1p, b1, fr1, w2, b2, fr2, w3, absdeltas)


def _colmm_kernel(w_ref, x_ref, o_ref, *, split):
    if split:
        o = _dot3(w_ref[...], x_ref[...])
    else:
        o = _dot(w_ref[...], x_ref[...])
    o_ref[...] = o.astype(o_ref.dtype)


def _colmm(w, x, out_dtype, split=False):
    p, k, ncols = x.shape
    m = w.shape[0]
    tc = min(COL_TILE, ncols)
    return pl.pallas_call(
        functools.partial(_colmm_kernel, split=split),
        out_shape=jax.ShapeDtypeStruct((p, m, ncols), out_dtype),
        grid=(p, ncols // tc),
        in_specs=[pl.BlockSpec((m, k), lambda pi, j: (0, 0)),
                  pl.BlockSpec((None, k, tc), lambda pi, j: (pi, 0, j))],
        out_specs=pl.BlockSpec((None, m, tc), lambda pi, j: (pi, 0, j)),
        compiler_params=pltpu.CompilerParams(dimension_semantics=("arbitrary", "arbitrary")),
        name="dft_outer",
    )(w, x)


def _stacked_g(f2r, f2i, tr, ti):
    gr = f2r * tr - f2i * ti
    gi = f2r * ti + f2i * tr
    return gr, gi


def _stack(a, b, c, d):
    return jnp.concatenate([jnp.concatenate([a, b], axis=1), jnp.concatenate([c, d], axis=1)], axis=0)


def _mid_data_kernel(a_ref, kf_ref, f2r_ref, f2i_ref, twr_ref, twi_ref, o_ref):
    f2r = f2r_ref[...]
    f2i = f2i_ref[...]
    for j in range(MID_KB):
        gr, gi = _stacked_g(f2r, f2i, twr_ref[j:j + 1, :], twi_ref[j:j + 1, :])
        gs = _stack(gr, -gi, gi, gr).astype(BF16)
        a = jnp.concatenate([a_ref[0, j], a_ref[1, j]], axis=0)
        x = _dot(gs, a)
        xr, xi = x[:DFT_N2], x[DFT_N2:]
        kr, ki = kf_ref[0, j], kf_ref[1, j]
        y = jnp.concatenate([xr * kr - xi * ki, xr * ki + xi * kr], axis=0).astype(BF16)
        grt, git = gr.T, gi.T
        gh = _stack(grt, git, -git, grt).astype(BF16)
        bp = _dot(gh, y)
        o_ref[0, j] = bp[:DFT_N2].astype(o_ref.dtype)
        o_ref[1, j] = bp[DFT_N2:].astype(o_ref.dtype)


def _mid_filter_kernel(a_ref, f2r_ref, f2i_ref, twr_ref, twi_ref, o_ref):
    f2r = f2r_ref[...]
    f2i = f2i_ref[...]
    for j in range(MID_KB):
        gr, gi = _stacked_g(f2r, f2i, twr_ref[j:j + 1, :], twi_ref[j:j + 1, :])
        gs = _stack(gr, -gi, gi, gr)
        a = jnp.concatenate([a_ref[0, j], a_ref[1, j]], axis=0)
        x = _dot3(gs, a)
        o_ref[0, j] = x[:DFT_N2]
        o_ref[1, j] = x[DFT_N2:]


def _mid_specs(n1):
    sq = pl.BlockSpec((DFT_N2, DFT_N2), lambda k, p: (0, 0))
    tw = pl.BlockSpec((MID_KB, DFT_N2), lambda k, p: (k, 0))
    return sq, tw


def _mid_data(a5, kf, f2r, f2i, twr, twi):
    p, _, n1, n2, c = a5.shape
    sq, tw = _mid_specs(n1)
    blk = pl.BlockSpec((None, 2, MID_KB, n2, c), lambda k, pi: (pi, 0, k, 0, 0))
    return pl.pallas_call(
        _mid_data_kernel,
        out_shape=jax.ShapeDtypeStruct(a5.shape, BF16),
        grid=(n1 // MID_KB, p),
        in_specs=[blk, pl.BlockSpec((2, MID_KB, n2, c), lambda k, pi: (0, k, 0, 0)), sq, sq, tw, tw],
        out_specs=blk,
        compiler_params=pltpu.CompilerParams(dimension_semantics=("arbitrary", "arbitrary")),
        name="dft_mid",
    )(a5, kf, f2r, f2i, twr, twi)


def _mid_filter(a4, f2r, f2i, twr, twi):
    _, n1, n2, c = a4.shape
    sq, tw = _mid_specs(n1)
    blk = pl.BlockSpec((2, MID_KB, n2, c), lambda k, pi: (0, k, 0, 0))
    return pl.pallas_call(
        _mid_filter_kernel,
        out_shape=jax.ShapeDtypeStruct(a4.shape, F32),
        grid=(n1 // MID_KB, 1),
        in_specs=[blk, sq, sq, tw, tw],
        out_specs=blk,
        compiler_params=pltpu.CompilerParams(dimension_semantics=("arbitrary", "arbitrary")),
        name="dft_mid_filter",
    )(a4, f2r, f2i, twr, twi)


def _attn_kernel(q_ref, k_ref, v_ref, bias_ref, o_ref, *, rows):
    rb = pl.program_id(1)
    lane = lax.broadcasted_iota(jnp.int32, (GRID_W, 128), 1)
    first = lane < HEAD_DIM
    band = WIN_ROWS * GRID_W

    def row_body(rl, carry):
        r = rb * ATT_ROWS + rl
        r0 = jnp.clip(r - WIN_ROWS // 2, 0, rows - WIN_ROWS)
        d = r - r0
        start = pl.multiple_of(r0 * GRID_W, GRID_W)
        qoff = pl.multiple_of(rl * GRID_W, GRID_W)
        for pr in range(N_HEADS // 2):
            cs = slice(pr * 128, (pr + 1) * 128)
            q2 = q_ref[pl.ds(qoff, GRID_W), cs]
            zero = jnp.zeros_like(q2)
            qbd = jnp.concatenate([jnp.where(first, q2, zero), jnp.where(first, zero, q2)], axis=0)
            kb = k_ref[pl.ds(start, band), cs]
            s = lax.dot_general(qbd, kb, (((1,), (1,)), ((), ())), preferred_element_type=F32)
            s = s + bias_ref[d, pr]
            m = jnp.max(s, axis=-1, keepdims=True)
            p = jnp.exp(s - m)
            den = jnp.sum(p, axis=-1, keepdims=True)
            vb = v_ref[pl.ds(start, band), cs]
            o = _dot(p.astype(BF16), vb) / den
            o_ref[pl.ds(qoff, GRID_W), cs] = jnp.where(first, o[:GRID_W], o[GRID_W:]).astype(o_ref.dtype)
        return carry

    lax.fori_loop(0, ATT_ROWS, row_body, 0)


def _attention(z3, bias_tab):
    b, l, _ = z3.shape
    rows = l // GRID_W
    tq = ATT_ROWS * GRID_W
    return pl.pallas_call(
        functools.partial(_attn_kernel, rows=rows),
        out_shape=jax.ShapeDtypeStruct((b, l, ATT_W), BF16),
        grid=(b, rows // ATT_ROWS),
        in_specs=[
            pl.BlockSpec((None, tq, CB), lambda bi, i: (bi, i, 4)),
            pl.BlockSpec((None, l, CB), lambda bi, i: (bi, 0, 5)),
            pl.BlockSpec((None, l, CB), lambda bi, i: (bi, 0, 6)),
            pl.BlockSpec(bias_tab.shape, lambda bi, i: (0, 0, 0, 0)),
        ],
        out_specs=pl.BlockSpec((None, tq, ATT_W), lambda bi, i: (bi, i, 0)),
        compiler_params=pltpu.CompilerParams(dimension_semantics=("arbitrary", "arbitrary")),
        name="nbr_attention",
    )(z3, z3, z3, bias_tab)


def _outproj_kernel(x_ref, yl_ref, u_ref, x0_ref, gh_ref, at_ref, ga_ref, hb_ref, onh_ref, ona_ref, w_ref, o_ref):
    yh = x0_ref[...].astype(F32) * (yl_ref[...] + u_ref[...].astype(F32) * hb_ref[...])
    yh = yh * lax.rsqrt(jnp.mean(yh * yh, axis=-1, keepdims=True) + EPS) * onh_ref[...]
    yh = yh * jax.nn.silu(gh_ref[...].astype(F32))
    ya = at_ref[...].astype(F32)
    ya = ya * lax.rsqrt(jnp.mean(ya * ya, axis=-1, keepdims=True) + EPS) * ona_ref[...]
    ya = ya * jax.nn.silu(ga_ref[...].astype(F32))
    acc = _dot(yh.astype(BF16), w_ref[:HY_W, :]) + _dot(ya.astype(BF16), w_ref[HY_W:, :])
    o_ref[...] = x_ref[...] + acc


def _outproj(x2, ylong, u, x0c, z2, att, hy_bias, on_hy, on_att, w_out_bf):
    t = x2.shape[0]
    tm = TOKEN_TILE
    row = lambda c: pl.BlockSpec((tm, CB), lambda i, c=c: (i, c))
    vec = pl.BlockSpec((1, CB), lambda i: (0, 0))
    return pl.pallas_call(
        _outproj_kernel,
        out_shape=jax.ShapeDtypeStruct((t, D_MODEL), F32),
        grid=(t // tm,),
        in_specs=[pl.BlockSpec((tm, D_MODEL), lambda i: (i, 0)), row(0), row(0), row(0), row(3), row(0), row(7),
                  vec, vec, vec, pl.BlockSpec((HY_W + ATT_W, D_MODEL), lambda i: (0, 0))],
        out_specs=pl.BlockSpec((tm, D_MODEL), lambda i: (i, 0)),
        compiler_params=pltpu.CompilerParams(dimension_semantics=("arbitrary",)),
        name="outproj",
    )(x2, ylong, u, x0c, z2, att, z2, hy_bias, on_hy, on_att, w_out_bf)


def _dft_tables(l):
    n = 2 * l
    n2 = DFT_N2
    n1 = n // n2
    h = n1 // 2
    k1 = np.arange(n1)
    ang1 = -2.0 * np.pi * ((k1[:, None] * k1[None, :]) % n1) / n1
    f1r, f1i = np.cos(ang1), np.sin(ang1)
    w_data = np.block([[f1r[:, :h], -f1i[:, :h]], [f1i[:, :h], f1r[:, :h]]])
    w_filt = np.concatenate([f1r, f1i], axis=0)
    w_inv = np.block([[f1r[:h], f1i[:h]], [-f1i[:h], f1r[:h]]]) / n
    k2 = np.arange(n2)
    ang2 = -2.0 * np.pi * ((k2[:, None] * k2[None, :]) % n2) / n2
    angt = -2.0 * np.pi * (k1[:, None] * k2[None, :]) / n
    f = lambda a: jnp.asarray(a, dtype=F32)
    return dict(n1=n1, w_data=f(w_data), w_filt=f(w_filt), w_inv=f(w_inv),
                f2r=f(np.cos(ang2)), f2i=f(np.sin(ang2)), twr=f(np.cos(angt)), twi=f(np.sin(angt)))


def _positional_features(l):
    t01 = jnp.linspace(0.0, 1.0, l, dtype=F32)[:, None]
    w = 2.0 * math.pi * jnp.arange(l, dtype=F32)[:, None] / l
    f = jnp.linspace(1e-4, POS_BANDS - 1, POS_BANDS, dtype=F32)[None, :]
    z = jnp.concatenate([t01, jnp.cos(f * w), -jnp.sin(f * w)], axis=-1)
    z2 = jnp.concatenate([z, jnp.zeros((1, z.shape[1]), F32), z[::-1][:l - 1]], axis=0)
    return jnp.pad(z2, ((0, 0), (0, 128 - z2.shape[1])))


def _abs_deltas():
    min_decay = math.log(1e-2) / 1.5
    max_decay = math.log(1e-2) / 0.3
    return jnp.abs(jnp.linspace(min_decay, max_decay, 2 * HY_W, dtype=F32))[None, :]


def _bias_table(rpb):
    cols = np.arange(GRID_W)
    cstart = np.clip(cols - WIN_COLS // 2, 0, GRID_W - WIN_COLS)
    kc = np.arange(GRID_W)
    valid = (kc[None, :] >= cstart[:, None]) & (kc[None, :] < cstart[:, None] + WIN_COLS)
    rel = np.clip(kc[None, :] - cols[:, None] + (WIN_COLS - 1), 0, 2 * WIN_COLS - 2)
    d = np.arange(WIN_ROWS)
    k = np.arange(WIN_ROWS)
    rowidx = k[None, :] - d[:, None] + (WIN_ROWS - 1)
    t = rpb.astype(F32)[:, rowidx][:, :, :, rel]
    t = jnp.where(valid[None, None, None], t, NEG_INF)
    t = t.transpose(1, 0, 3, 2, 4)
    return t.reshape(WIN_ROWS, N_HEADS // 2, 2 * GRID_W, WIN_ROWS * GRID_W)


def _filter_spectrum(l, tabs, zpos2, absd, f_w1, f_b1, f_fr1, f_w2, f_b2, f_fr2, f_w3):
    n1 = tabs["n1"]
    w1p = jnp.pad(f_w1, ((0, 128 - f_w1.shape[0]), (0, 0)))
    kfilt = _filter(zpos2, w1p, f_b1[None], f_fr1[None], f_w2, f_b2[None], f_fr2[None], f_w3, absd, l)
    a = _colmm(tabs["w_filt"], kfilt.reshape(1, n1, DFT_N2 * HY_W), F32, split=True)
    return _mid_filter(a.reshape(2, n1, DFT_N2, HY_W), tabs["f2r"], tabs["f2i"], tabs["twr"], tabs["twi"])


def _layer(x, tabs, kf, bias_tab, bd, norm_g, w_in, conv_w, conv_b, hy_bias, qn_g, kn_g, on_hy, on_att, w_out):
    b, l, _ = x.shape
    t = b * l
    n1 = tabs["n1"]
    x2 = x.reshape(t, D_MODEL)
    qgain = jnp.tile(qn_g, N_HEADS)[None] * (HEAD_DIM ** -0.5)
    kgain = jnp.tile(kn_g, N_HEADS)[None]
    z2 = _inproj(x2, norm_g[None], w_in.astype(BF16), qgain, kgain, bd)
    z3 = z2.reshape(b, l, N_IN)
    u, x0c = _hypre(z3, conv_w, conv_b[None])
    a = _colmm(tabs["w_data"].astype(BF16), u.reshape(b // 2, n1, DFT_N2 * HY_W), BF16)
    bp = _mid_data(a.reshape(b // 2, 2, n1, DFT_N2, HY_W), kf, tabs["f2r"], tabs["f2i"], tabs["twr"], tabs["twi"])
    ylong = _colmm(tabs["w_inv"].astype(BF16), bp.reshape(b // 2, 2 * n1, DFT_N2 * HY_W), F32)
    att = _attention(z3, bias_tab)
    out = _outproj(x2, ylong.reshape(t, HY_W), u.reshape(t, HY_W), x0c.reshape(t, HY_W), z2,
                   att.reshape(t, ATT_W), hy_bias[None], on_hy[None], on_att[None], w_out.astype(BF16))
    return out.reshape(b, l, D_MODEL)


def _trunk(x, norm_g, w_in, conv_w, conv_b, f_w1, f_b1, f_fr1, f_w2, f_b2, f_fr2, f_w3,
           hy_bias, qn_g, kn_g, rpb, on_hy, on_att, w_out):
    b, l, _ = x.shape
    assert b % 2 == 0 and l % (GRID_W * ATT_ROWS) == 0 and l // GRID_W >= WIN_ROWS
    assert (2 * l) % (DFT_N2 * MID_KB) == 0 and l % TOKEN_TILE == 0
    tabs = _dft_tables(l)
    zpos2 = _positional_features(l)
    absd = _abs_deltas()
    head = np.arange(CB) // HEAD_DIM
    bd = jnp.asarray((head[:, None] == head[None, :]).astype(np.float32) / HEAD_DIM, dtype=BF16)
    for i in range(norm_g.shape[0]):
        kf = _filter_spectrum(l, tabs, zpos2, absd, f_w1[i], f_b1[i], f_fr1[i], f_w2[i], f_b2[i], f_fr2[i], f_w3[i])
        x = _layer(x, tabs, kf, _bias_table(rpb[i]), bd, norm_g[i], w_in[i], conv_w[i], conv_b[i], hy_bias[i],
                   qn_g[i], kn_g[i], on_hy[i], on_att[i], w_out[i])
    return x


def kernel(x_prompt, x_sample, norm_g, w_in, conv_w, conv_b, f_w1, f_b1, f_fr1, f_w2, f_b2, f_fr2, f_w3,
           hy_bias, qn_g, kn_g, rpb, on_hy, on_att, w_out):
    params = (norm_g, w_in, conv_w, conv_b, f_w1, f_b1, f_fr1, f_w2, f_b2, f_fr2, f_w3,
              hy_bias, qn_g, kn_g, rpb, on_hy, on_att, w_out)
    return (_trunk(x_prompt, *params), _trunk(x_sample, *params))
```

```python
import functools
import math

import numpy as np
import jax
import jax.numpy as jnp
from jax import lax
from jax.experimental import pallas as pl
from jax.experimental.pallas import tpu as pltpu

F32 = jnp.float32
BF16 = jnp.bfloat16

D_MODEL = 1024
GRID_W = 64
HY_W = 512
ATT_W = 512
HEAD_DIM = 64
N_HEADS = 8
WIN_ROWS = 8
WIN_COLS = 16
POS_BANDS = 16
FILTER_HIDDEN = 64
N_IN = 4 * HY_W + 4 * ATT_W
EPS = 1e-6
NEG_INF = -1e30

CB = 512
DFT_N2 = 128
TOKEN_TILE = 512
HALO = 16
ATT_ROWS = 8
KEY_CHUNK = 256
KT_UNIT = 128
ATT_UNROLL = 2
LOG2E = math.log2(math.e)
MID_KB = 8
COL_TILE = 4096


def _dot(a, b):
    return jnp.dot(a, b, preferred_element_type=F32)


def _dot_hi(a, b):
    return jnp.dot(a, b, preferred_element_type=F32, precision=lax.Precision.HIGHEST)


def _split(x):
    hi = x.astype(BF16)
    lo = (x - hi.astype(F32)).astype(BF16)
    return hi, lo


def _dot3(a, b):
    ah, al = _split(a)
    bh, bl = _split(b)
    return _dot(ah, bh) + (_dot(ah, bl) + _dot(al, bh))


def _inproj_kernel(x_ref, g_ref, w_ref, qg_ref, kg_ref, bd_ref, z_ref):
    x = x_ref[...]
    ms = jnp.mean(x * x, axis=-1, keepdims=True)
    h = (x * lax.rsqrt(ms + EPS) * g_ref[...]).astype(BF16)
    for j in range(N_IN // CB):
        zj = _dot(h, w_ref[:, j * CB:(j + 1) * CB])
        if j in (4, 5):
            hm = _dot((zj * zj).astype(BF16), bd_ref[...])
            gain = qg_ref[...] if j == 4 else kg_ref[...]
            zj = zj * lax.rsqrt(hm + EPS) * gain
        z_ref[:, j * CB:(j + 1) * CB] = zj.astype(z_ref.dtype)


def _inproj(x2, norm_g, w_in_bf, qgain, kgain, bd):
    t = x2.shape[0]
    tm = TOKEN_TILE
    const = lambda i: (0, 0)
    return pl.pallas_call(
        _inproj_kernel,
        out_shape=jax.ShapeDtypeStruct((t, N_IN), BF16),
        grid=(t // tm,),
        in_specs=[
            pl.BlockSpec((tm, D_MODEL), lambda i: (i, 0)),
            pl.BlockSpec((1, D_MODEL), const),
            pl.BlockSpec((D_MODEL, N_IN), const),
            pl.BlockSpec((1, CB), const),
            pl.BlockSpec((1, CB), const),
            pl.BlockSpec((CB, CB), const),
        ],
        out_specs=pl.BlockSpec((tm, N_IN), lambda i: (i, 0)),
        compiler_params=pltpu.CompilerParams(dimension_semantics=("arbitrary",)),
        name="inproj",
    )(x2, norm_g, w_in_bf, qgain, kgain, bd)


def _hypre_kernel(c0, c1, c2, p0, p1, p2, n0, n1, n2, cw_ref, cb_ref, u_ref, x0_ref, *, tl):
    i = pl.program_id(1)
    last = pl.num_programs(1) - 1
    rows = lax.broadcasted_iota(jnp.int32, (tl, CB), 0)

    def conv(cur_ref, prev_ref, next_ref, c):
        cur = cur_ref[...].astype(F32)
        prev_row = jnp.where(i > 0, prev_ref[HALO - 1:HALO, :].astype(F32), 0.0)
        next_row = jnp.where(i < last, next_ref[0:1, :].astype(F32), 0.0)
        dn = jnp.where(rows == 0, prev_row, pltpu.roll(cur, 1, 0))
        up = jnp.where(rows == tl - 1, next_row, pltpu.roll(cur, tl - 1, 0))
        w = cw_ref[:, c * CB:(c + 1) * CB]
        b = cb_ref[:, c * CB:(c + 1) * CB]
        return w[0:1, :] * dn + w[1:2, :] * cur + w[2:3, :] * up + b

    x0_ref[...] = conv(c0, p0, n0, 0).astype(x0_ref.dtype)
    u_ref[...] = (conv(c1, p1, n1, 1) * conv(c2, p2, n2, 2)).astype(u_ref.dtype)


def _hypre(z3, conv_w, conv_b):
    b, l, _ = z3.shape
    tl = TOKEN_TILE
    hb = tl // HALO
    nh = l // HALO
    cur = lambda c: pl.BlockSpec((None, tl, CB), lambda bi, i, c=c: (bi, i, c))
    prv = lambda c: pl.BlockSpec((None, HALO, CB), lambda bi, i, c=c: (bi, jnp.maximum(i * hb - 1, 0), c))
    nxt = lambda c: pl.BlockSpec((None, HALO, CB), lambda bi, i, c=c: (bi, jnp.minimum((i + 1) * hb, nh - 1), c))
    const = lambda bi, i: (0, 0)
    out = pl.BlockSpec((None, tl, CB), lambda bi, i: (bi, i, 0))
    return pl.pallas_call(
        functools.partial(_hypre_kernel, tl=tl),
        out_shape=(jax.ShapeDtypeStruct((b, l, CB), BF16), jax.ShapeDtypeStruct((b, l, CB), BF16)),
        grid=(b, l // tl),
        in_specs=[cur(0), cur(1), cur(2), prv(0), prv(1), prv(2), nxt(0), nxt(1), nxt(2),
                  pl.BlockSpec((3, 3 * HY_W), const), pl.BlockSpec((1, 3 * HY_W), const)],
        out_specs=(out, out),
        compiler_params=pltpu.CompilerParams(dimension_semantics=("arbitrary", "arbitrary")),
        name="hyena_pre",
    )(z3, z3, z3, z3, z3, z3, z3, z3, z3, conv_w, conv_b)


def _filter_kernel(z_ref, w1_ref, b1_ref, fr1_ref, w2_ref, b2_ref, fr2_ref, w3_ref, dl_ref, o_ref, *, l, tn):
    i = pl.program_id(0)
    z = z_ref[...]
    h = jnp.sin(fr1_ref[...] * (_dot_hi(z, w1_ref[...]) + b1_ref[...]))
    h = jnp.sin(fr2_ref[...] * (_dot_hi(h, w2_ref[...]) + b2_ref[...]))
    h = _dot_hi(h, w3_ref[...])
    h = h * jnp.exp(-z[:, 0:1] * dl_ref[...])
    row = i * tn + lax.broadcasted_iota(jnp.int32, h.shape, 0)
    o_ref[...] = jnp.where(row == l, 0.0, h)


def _filter(zpos2, w1p, b1, fr1, w2, b2, fr2, w3, absdeltas, l):
    n = 2 * l
    tn = min(512, l)
    nb_half = l // tn
    const = lambda i: (0, 0)
    half = lambda i: (0, i // nb_half)
    return pl.pallas_call(
        functools.partial(_filter_kernel, l=l, tn=tn),
        out_shape=jax.ShapeDtypeStruct((n, HY_W), F32),
        grid=(n // tn,),
        in_specs=[
            pl.BlockSpec((tn, 128), lambda i: (i, 0)),
            pl.BlockSpec((128, FILTER_HIDDEN), const),
            pl.BlockSpec((1, FILTER_HIDDEN), const),
            pl.BlockSpec((1, FILTER_HIDDEN), const),
            pl.BlockSpec((FILTER_HIDDEN, FILTER_HIDDEN), const),
            pl.BlockSpec((1, FILTER_HIDDEN), const),
            pl.BlockSpec((1, FILTER_HIDDEN), const),
            pl.BlockSpec((FILTER_HIDDEN, HY_W), half),
            pl.BlockSpec((1, HY_W), half),
        ],
        out_specs=pl.BlockSpec((tn, HY_W), lambda i: (i, 0)),
        compiler_params=pltpu.CompilerParams(dimension_semantics=("arbitrary",)),
        name="hyena_filter",
    )(zpos2, w1p, b1, fr1, w2, b2, fr2, w3, absdeltas)


def _colmm_kernel(w_ref, x_ref, o_ref, *, split):
    if split:
        o = _dot3(w_ref[...], x_ref[...])
    else:
        o = _dot(w_ref[...], x_ref[...])
    o_ref[...] = o.astype(o_ref.dtype)


def _colmm(w, x, out_dtype, split=False):
    p, k, ncols = x.shape
    m = w.shape[0]
    tc = min(COL_TILE, ncols)
    return pl.pallas_call(
        functools.partial(_colmm_kernel, split=split),
        out_shape=jax.ShapeDtypeStruct((p, m, ncols), out_dtype),
        grid=(p, ncols // tc),
        in_specs=[pl.BlockSpec((m, k), lambda pi, j: (0, 0)),
                  pl.BlockSpec((None, k, tc), lambda pi, j: (pi, 0, j))],
        out_specs=pl.BlockSpec((None, m, tc), lambda pi, j: (pi, 0, j)),
        compiler_params=pltpu.CompilerParams(dimension_semantics=("arbitrary", "arbitrary")),
        name="dft_outer",
    )(w, x)


def _stacked_g(f2r, f2i, tr, ti):
    gr = f2r * tr - f2i * ti
    gi = f2r * ti + f2i * tr
    return gr, gi


def _stack(a, b, c, d):
    return jnp.concatenate([jnp.concatenate([a, b], axis=1), jnp.concatenate([c, d], axis=1)], axis=0)


def _mid_data_kernel(a_ref, kf_ref, f2r_ref, f2i_ref, twr_ref, twi_ref, o_ref):
    f2r = f2r_ref[...]
    f2i = f2i_ref[...]
    for j in range(MID_KB):
        gr, gi = _stacked_g(f2r, f2i, twr_ref[j:j + 1, :], twi_ref[j:j + 1, :])
        gs = _stack(gr, -gi, gi, gr).astype(BF16)
        a = jnp.concatenate([a_ref[0, j], a_ref[1, j]], axis=0)
        x = _dot(gs, a)
        xr, xi = x[:DFT_N2], x[DFT_N2:]
        kr, ki = kf_ref[0, j], kf_ref[1, j]
        y = jnp.concatenate([xr * kr - xi * ki, xr * ki + xi * kr], axis=0).astype(BF16)
        grt, git = gr.T, gi.T
        gh = _stack(grt, git, -git, grt).astype(BF16)
        bp = _dot(gh, y)
        o_ref[0, j] = bp[:DFT_N2].astype(o_ref.dtype)
        o_ref[1, j] = bp[DFT_N2:].astype(o_ref.dtype)


def _mid_filter_kernel(a_ref, f2r_ref, f2i_ref, twr_ref, twi_ref, o_ref):
    f2r = f2r_ref[...]
    f2i = f2i_ref[...]
    for j in range(MID_KB):
        gr, gi = _stacked_g(f2r, f2i, twr_ref[j:j + 1, :], twi_ref[j:j + 1, :])
        gs = _stack(gr, -gi, gi, gr)
        a = jnp.concatenate([a_ref[0, j], a_ref[1, j]], axis=0)
        x = _dot3(gs, a)
        o_ref[0, j] = x[:DFT_N2]
        o_ref[1, j] = x[DFT_N2:]


def _mid_specs(n1):
    sq = pl.BlockSpec((DFT_N2, DFT_N2), lambda k, p: (0, 0))
    tw = pl.BlockSpec((MID_KB, DFT_N2), lambda k, p: (k, 0))
    return sq, tw


def _mid_data(a5, kf, f2r, f2i, twr, twi):
    p, _, n1, n2, c = a5.shape
    sq, tw = _mid_specs(n1)
    blk = pl.BlockSpec((None, 2, MID_KB, n2, c), lambda k, pi: (pi, 0, k, 0, 0))
    return pl.pallas_call(
        _mid_data_kernel,
        out_shape=jax.ShapeDtypeStruct(a5.shape, BF16),
        grid=(n1 // MID_KB, p),
        in_specs=[blk, pl.BlockSpec((2, MID_KB, n2, c), lambda k, pi: (0, k, 0, 0)), sq, sq, tw, tw],
        out_specs=blk,
        compiler_params=pltpu.CompilerParams(dimension_semantics=("arbitrary", "arbitrary")),
        name="dft_mid",
    )(a5, kf, f2r, f2i, twr, twi)


def _mid_filter(a4, f2r, f2i, twr, twi):
    _, n1, n2, c = a4.shape
    sq, tw = _mid_specs(n1)
    blk = pl.BlockSpec((2, MID_KB, n2, c), lambda k, pi: (0, k, 0, 0))
    return pl.pallas_call(
        _mid_filter_kernel,
        out_shape=jax.ShapeDtypeStruct(a4.shape, F32),
        grid=(n1 // MID_KB, 1),
        in_specs=[blk, sq, sq, tw, tw],
        out_specs=blk,
        compiler_params=pltpu.CompilerParams(dimension_semantics=("arbitrary", "arbitrary")),
        name="dft_mid_filter",
    )(a4, f2r, f2i, twr, twi)


def _attn_kernel(q_ref, k_ref, v_ref, bias_ref, o_ref, kt_scr, s_scr, *, rows):
    rb = pl.program_id(1)
    lane = lax.broadcasted_iota(jnp.int32, (GRID_W, 128), 1)
    first = lane < HEAD_DIM
    ones = jnp.ones((KEY_CHUNK, 128), BF16)
    band = WIN_ROWS * GRID_W
    n_chunks = band // KEY_CHUNK
    n_keys = rows * GRID_W
    units_per_block = ATT_ROWS * GRID_W // KT_UNIT
    n_trips = ATT_ROWS // ATT_UNROLL
    assert units_per_block == n_trips
    pairs = range(N_HEADS // 2)

    def transpose_unit(m):
        lanes = pl.ds(pl.multiple_of(m * KT_UNIT, KT_UNIT), KT_UNIT)
        kt_scr[0, :, lanes] = k_ref[pl.ds(pl.multiple_of(m * KT_UNIT, KT_UNIT), KT_UNIT), :].T
        src = jnp.minimum(m * KT_UNIT + GRID_W, n_keys - KT_UNIT)
        kt_scr[1, :, lanes] = k_ref[pl.ds(pl.multiple_of(src, GRID_W), KT_UNIT), :].T

    @pl.when(rb == 0)
    def _():
        for j in range(units_per_block):
            transpose_unit(j)

    nxt = jnp.minimum(rb + 1, pl.num_programs(1) - 1)

    def trip(t, carry):
        transpose_unit(nxt * units_per_block + t)
        info = []
        for a in range(ATT_UNROLL):
            rl = t * ATT_UNROLL + a
            r = rb * ATT_ROWS + rl
            r0 = jnp.clip(r - WIN_ROWS // 2, 0, rows - WIN_ROWS)
            par = r0 % 2
            info.append((r - r0, par, (r0 - par) * GRID_W, r0 * GRID_W, pl.multiple_of(rl * GRID_W, GRID_W)))
        maxes = {}
        for a, (d, par, kt_start, start, qoff) in enumerate(info):
            for pr in pairs:
                cs = slice(pr * 128, (pr + 1) * 128)
                q2 = q_ref[pl.ds(qoff, GRID_W), cs]
                zero = jnp.zeros_like(q2)
                qbd = jnp.concatenate([jnp.where(first, q2, zero), jnp.where(first, zero, q2)], axis=0)
                macc = None
                for c in range(n_chunks):
                    kl = pl.ds(pl.multiple_of(kt_start + c * KEY_CHUNK, 128), KEY_CHUNK)
                    lanes = slice(c * KEY_CHUNK, (c + 1) * KEY_CHUNK)
                    s = _dot(qbd, kt_scr[par, cs, kl]) + bias_ref[d, pr, :, lanes]
                    s_scr[a, pr, :, lanes] = s
                    macc = s if macc is None else jnp.maximum(macc, s)
                maxes[a, pr] = jnp.max(macc, axis=-1, keepdims=True)
        for a, (d, par, kt_start, start, qoff) in enumerate(info):
            for pr in pairs:
                cs = slice(pr * 128, (pr + 1) * 128)
                acc = jnp.zeros((2 * GRID_W, 256), F32)
                for c in range(n_chunks):
                    ks = pl.ds(pl.multiple_of(start + c * KEY_CHUNK, GRID_W), KEY_CHUNK)
                    lanes = slice(c * KEY_CHUNK, (c + 1) * KEY_CHUNK)
                    p = jnp.exp2(s_scr[a, pr, :, lanes] - maxes[a, pr]).astype(BF16)
                    acc = acc + _dot(p, jnp.concatenate([v_ref[ks, cs], ones], axis=1))
                o = acc[:, :128] / acc[:, 128:]
                o_ref[pl.ds(qoff, GRID_W), cs] = jnp.where(first, o[:GRID_W], o[GRID_W:]).astype(o_ref.dtype)
        return carry

    lax.fori_loop(0, n_trips, trip, 0)


def _attention(z3, bias_tab):
    b, l, _ = z3.shape
    rows = l // GRID_W
    tq = ATT_ROWS * GRID_W
    return pl.pallas_call(
        functools.partial(_attn_kernel, rows=rows),
        out_shape=jax.ShapeDtypeStruct((b, l, ATT_W), BF16),
        grid=(b, rows // ATT_ROWS),
        in_specs=[
            pl.BlockSpec((None, tq, CB), lambda bi, i: (bi, i, 4)),
            pl.BlockSpec((None, l, CB), lambda bi, i: (bi, 0, 5), pipeline_mode=pl.Buffered(1)),
            pl.BlockSpec((None, l, CB), lambda bi, i: (bi, 0, 6)),
            pl.BlockSpec(bias_tab.shape, lambda bi, i: (0, 0, 0, 0), pipeline_mode=pl.Buffered(1)),
        ],
        out_specs=pl.BlockSpec((None, tq, ATT_W), lambda bi, i: (bi, i, 0)),
        scratch_shapes=[pltpu.VMEM((2, ATT_W, l), BF16),
                        pltpu.VMEM((ATT_UNROLL, N_HEADS // 2, 2 * GRID_W, WIN_ROWS * GRID_W), F32)],
        compiler_params=pltpu.CompilerParams(dimension_semantics=("arbitrary", "arbitrary")),
        name="nbr_attention",
    )(z3, z3, z3, bias_tab)


def _outproj_kernel(x_ref, yl_ref, u_ref, x0_ref, gh_ref, at_ref, ga_ref, hb_ref, onh_ref, ona_ref, w_ref, o_ref):
    yh = x0_ref[...].astype(F32) * (yl_ref[...] + u_ref[...].astype(F32) * hb_ref[...])
    yh = yh * lax.rsqrt(jnp.mean(yh * yh, axis=-1, keepdims=True) + EPS) * onh_ref[...]
    yh = yh * jax.nn.silu(gh_ref[...].astype(F32))
    ya = at_ref[...].astype(F32)
    ya = ya * lax.rsqrt(jnp.mean(ya * ya, axis=-1, keepdims=True) + EPS) * ona_ref[...]
    ya = ya * jax.nn.silu(ga_ref[...].astype(F32))
    acc = _dot(yh.astype(BF16), w_ref[:HY_W, :]) + _dot(ya.astype(BF16), w_ref[HY_W:, :])
    o_ref[...] = x_ref[...] + acc


def _outproj(x2, ylong, u, x0c, z2, att, hy_bias, on_hy, on_att, w_out_bf):
    t = x2.shape[0]
    tm = TOKEN_TILE
    row = lambda c: pl.BlockSpec((tm, CB), lambda i, c=c: (i, c))
    vec = pl.BlockSpec((1, CB), lambda i: (0, 0))
    return pl.pallas_call(
        _outproj_kernel,
        out_shape=jax.ShapeDtypeStruct((t, D_MODEL), F32),
        grid=(t // tm,),
        in_specs=[pl.BlockSpec((tm, D_MODEL), lambda i: (i, 0)), row(0), row(0), row(0), row(3), row(0), row(7),
                  vec, vec, vec, pl.BlockSpec((HY_W + ATT_W, D_MODEL), lambda i: (0, 0))],
        out_specs=pl.BlockSpec((tm, D_MODEL), lambda i: (i, 0)),
        compiler_params=pltpu.CompilerParams(dimension_semantics=("arbitrary",)),
        name="outproj",
    )(x2, ylong, u, x0c, z2, att, z2, hy_bias, on_hy, on_att, w_out_bf)


def _dft_tables(l):
    n = 2 * l
    n2 = DFT_N2
    n1 = n // n2
    h = n1 // 2
    k1 = np.arange(n1)
    ang1 = -2.0 * np.pi * ((k1[:, None] * k1[None, :]) % n1) / n1
    f1r, f1i = np.cos(ang1), np.sin(ang1)
    w_data = np.block([[f1r[:, :h], -f1i[:, :h]], [f1i[:, :h], f1r[:, :h]]])
    w_filt = np.concatenate([f1r, f1i], axis=0)
    w_inv = np.block([[f1r[:h], f1i[:h]], [-f1i[:h], f1r[:h]]]) / n
    k2 = np.arange(n2)
    ang2 = -2.0 * np.pi * ((k2[:, None] * k2[None, :]) % n2) / n2
    angt = -2.0 * np.pi * (k1[:, None] * k2[None, :]) / n
    f = lambda a: jnp.asarray(a, dtype=F32)
    return dict(n1=n1, w_data=f(w_data), w_filt=f(w_filt), w_inv=f(w_inv),
                f2r=f(np.cos(ang2)), f2i=f(np.sin(ang2)), twr=f(np.cos(angt)), twi=f(np.sin(angt)))


def _positional_features(l):
    t01 = jnp.linspace(0.0, 1.0, l, dtype=F32)[:, None]
    w = 2.0 * math.pi * jnp.arange(l, dtype=F32)[:, None] / l
    f = jnp.linspace(1e-4, POS_BANDS - 1, POS_BANDS, dtype=F32)[None, :]
    z = jnp.concatenate([t01, jnp.cos(f * w), -jnp.sin(f * w)], axis=-1)
    z2 = jnp.concatenate([z, jnp.zeros((1, z.shape[1]), F32), z[::-1][:l - 1]], axis=0)
    return jnp.pad(z2, ((0, 0), (0, 128 - z2.shape[1])))


def _abs_deltas():
    min_decay = math.log(1e-2) / 1.5
    max_decay = math.log(1e-2) / 0.3
    return jnp.abs(jnp.linspace(min_decay, max_decay, 2 * HY_W, dtype=F32))[None, :]


def _bias_table(rpb):
    cols = np.arange(GRID_W)
    cstart = np.clip(cols - WIN_COLS // 2, 0, GRID_W - WIN_COLS)
    kc = np.arange(GRID_W)
    valid = (kc[None, :] >= cstart[:, None]) & (kc[None, :] < cstart[:, None] + WIN_COLS)
    rel = np.clip(kc[None, :] - cols[:, None] + (WIN_COLS - 1), 0, 2 * WIN_COLS - 2)
    d = np.arange(WIN_ROWS)
    k = np.arange(WIN_ROWS)
    rowidx = k[None, :] - d[:, None] + (WIN_ROWS - 1)
    t = rpb.astype(F32)[:, rowidx][:, :, :, rel]
    t = jnp.where(valid[None, None, None], t * LOG2E, NEG_INF)
    t = t.transpose(1, 0, 3, 2, 4)
    return t.reshape(WIN_ROWS, N_HEADS // 2, 2 * GRID_W, WIN_ROWS * GRID_W)


def _filter_spectrum(l, tabs, zpos2, absd, f_w1, f_b1, f_fr1, f_w2, f_b2, f_fr2, f_w3):
    n1 = tabs["n1"]
    w1p = jnp.pad(f_w1, ((0, 128 - f_w1.shape[0]), (0, 0)))
    kfilt = _filter(zpos2, w1p, f_b1[None], f_fr1[None], f_w2, f_b2[None], f_fr2[None], f_w3, absd, l)
    a = _colmm(tabs["w_filt"], kfilt.reshape(1, n1, DFT_N2 * HY_W), F32, split=True)
    return _mid_filter(a.reshape(2, n1, DFT_N2, HY_W), tabs["f2r"], tabs["f2i"], tabs["twr"], tabs["twi"])


def _layer(x, tabs, kf, bias_tab, bd, norm_g, w_in, conv_w, conv_b, hy_bias, qn_g, kn_g, on_hy, on_att, w_out):
    b, l, _ = x.shape
    t = b * l
    n1 = tabs["n1"]
    x2 = x.reshape(t, D_MODEL)
    qgain = jnp.tile(qn_g, N_HEADS)[None] * (HEAD_DIM ** -0.5 * LOG2E)
    kgain = jnp.tile(kn_g, N_HEADS)[None]
    z2 = _inproj(x2, norm_g[None], w_in.astype(BF16), qgain, kgain, bd)
    z3 = z2.reshape(b, l, N_IN)
    u, x0c = _hypre(z3, conv_w, conv_b[None])
    a = _colmm(tabs["w_data"].astype(BF16), u.reshape(b // 2, n1, DFT_N2 * HY_W), BF16)
    bp = _mid_data(a.reshape(b // 2, 2, n1, DFT_N2, HY_W), kf, tabs["f2r"], tabs["f2i"], tabs["twr"], tabs["twi"])
    ylong = _colmm(tabs["w_inv"].astype(BF16), bp.reshape(b // 2, 2 * n1, DFT_N2 * HY_W), F32)
    att = _attention(z3, bias_tab)
    out = _outproj(x2, ylong.reshape(t, HY_W), u.reshape(t, HY_W), x0c.reshape(t, HY_W), z2,
                   att.reshape(t, ATT_W), hy_bias[None], on_hy[None], on_att[None], w_out.astype(BF16))
    return out.reshape(b, l, D_MODEL)


def _trunk(x, norm_g, w_in, conv_w, conv_b, f_w1, f_b1, f_fr1, f_w2, f_b2, f_fr2, f_w3,
           hy_bias, qn_g, kn_g, rpb, on_hy, on_att, w_out):
    b, l, _ = x.shape
    assert b % 2 == 0 and l % (GRID_W * ATT_ROWS) == 0 and l // GRID_W >= WIN_ROWS
    assert (2 * l) % (DFT_N2 * MID_KB) == 0 and l % TOKEN_TILE == 0
    tabs = _dft_tables(l)
    zpos2 = _positional_features(l)
    absd = _abs_deltas()
    head = np.arange(CB) // HEAD_DIM
    bd = jnp.asarray((head[:, None] == head[None, :]).astype(np.float32) / HEAD_DIM, dtype=BF16)
    for i in range(norm_g.shape[0]):
        kf = _filter_spectrum(l, tabs, zpos2, absd, f_w1[i], f_b1[i], f_fr1[i], f_w2[i], f_b2[i], f_fr2[i], f_w3[i])
        x = _layer(x, tabs, kf, _bias_table(rpb[i]), bd, norm_g[i], w_in[i], conv_w[i], conv_b[i], hy_bias[i],
                   qn_g[i], kn_g[i], on_hy[i], on_att[i], w_out[i])
    return x


def kernel(x_prompt, x_sample, norm_g, w_in, conv_w, conv_b, f_w1, f_b1, f_fr1, f_w2, f_b2, f_fr2, f_w3,
           hy_bias, qn_g, kn_g, rpb, on_hy, on_att, w_out):
    params = (norm_g, w_in, conv_w, conv_b, f_w1, f_b1, f_fr1, f_w2, f_b2, f_fr2, f_w3,
              hy_bias, qn_g, kn_g, rpb, on_hy, on_att, w_out)
    return (_trunk(x_prompt, *params), _trunk(x_sample, *params))
```

```python
import functools
import math

import numpy as np
import jax
import jax.numpy as jnp
from jax import lax
from jax.experimental import pallas as pl
from jax.experimental.pallas import tpu as pltpu

F32 = jnp.float32
BF16 = jnp.bfloat16

D_MODEL = 1024
GRID_W = 64
HY_W = 512
ATT_W = 512
HEAD_DIM = 64
N_HEADS = 8
WIN_ROWS = 8
WIN_COLS = 16
POS_BANDS = 16
FILTER_HIDDEN = 64
N_IN = 4 * HY_W + 4 * ATT_W
EPS = 1e-6
NEG_INF = -1e30

CB = 512
DFT_N2 = 128
TOKEN_TILE = 512
N_HY_SLOTS = 3
Z_GATE_HY, Z_Q, Z_K, Z_V, Z_GATE_ATT = range(5)
ATT_ROWS = 8
KEY_CHUNK = 256
KT_UNIT = 128
ATT_UNROLL = 2
LOG2E = math.log2(math.e)
MID_KB = 8
SUBLANES = 8
SLAB_ROWS = 16


def _dot(a, b):
    return jnp.dot(a, b, preferred_element_type=F32)


def _dot_hi(a, b):
    return jnp.dot(a, b, preferred_element_type=F32, precision=lax.Precision.HIGHEST)


def _split(x):
    hi = x.astype(BF16)
    lo = (x - hi.astype(F32)).astype(BF16)
    return hi, lo


def _dot3(a, b):
    ah, al = _split(a)
    bh, bl = _split(b)
    return _dot(ah, bh) + (_dot(ah, bl) + _dot(al, bh))


def _inproj_kernel(x_ref, xp_ref, xn_ref, g_ref, w_ref, qg_ref, kg_ref, bd_ref, cw_ref, cb_ref,
                   z_ref, u_ref, x0_ref, *, tiles_per_seq):
    pos = pl.program_id(0) % tiles_per_seq
    tm = x_ref.shape[0]

    def norm(x):
        ms = jnp.mean(x * x, axis=-1, keepdims=True)
        return (x * lax.rsqrt(ms + EPS) * g_ref[...]).astype(BF16)

    h = norm(x_ref[...])
    hp = norm(xp_ref[...])
    hn = norm(xn_ref[...])
    rows = lax.broadcasted_iota(jnp.int32, (tm, CB), 0)
    conv = []
    for j in range(N_HY_SLOTS):
        cols = slice(j * CB, (j + 1) * CB)
        cur = _dot(h, w_ref[:, cols])
        prev_row = jnp.where(pos > 0, _dot(hp, w_ref[:, cols])[SUBLANES - 1:SUBLANES, :], 0.0)
        next_row = jnp.where(pos < tiles_per_seq - 1, _dot(hn, w_ref[:, cols])[0:1, :], 0.0)
        dn = jnp.where(rows == 0, prev_row, pltpu.roll(cur, 1, 0))
        up = jnp.where(rows == tm - 1, next_row, pltpu.roll(cur, tm - 1, 0))
        cw = cw_ref[:, cols]
        conv.append(cw[0:1, :] * dn + cw[1:2, :] * cur + cw[2:3, :] * up + cb_ref[:, cols])
    x0_ref[...] = conv[0].astype(x0_ref.dtype)
    u_ref[...] = (conv[1] * conv[2]).astype(u_ref.dtype)
    for j in range(N_HY_SLOTS, N_IN // CB):
        zj = _dot(h, w_ref[:, j * CB:(j + 1) * CB])
        if j in (4, 5):
            hm = _dot((zj * zj).astype(BF16), bd_ref[...])
            gain = qg_ref[...] if j == 4 else kg_ref[...]
            zj = zj * lax.rsqrt(hm + EPS) * gain
        z_ref[:, (j - N_HY_SLOTS) * CB:(j - N_HY_SLOTS + 1) * CB] = zj.astype(z_ref.dtype)


def _inproj(x2, seq_len, norm_g, w_in_bf, qgain, kgain, bd, conv_w, conv_b):
    t = x2.shape[0]
    tm = TOKEN_TILE
    hb = tm // SUBLANES
    nh = t // SUBLANES
    const = lambda i: (0, 0)
    row = lambda w: pl.BlockSpec((tm, w), lambda i: (i, 0))
    z_w = N_IN - N_HY_SLOTS * CB
    return pl.pallas_call(
        functools.partial(_inproj_kernel, tiles_per_seq=seq_len // tm),
        out_shape=(jax.ShapeDtypeStruct((t, z_w), BF16), jax.ShapeDtypeStruct((t, CB), BF16),
                   jax.ShapeDtypeStruct((t, CB), BF16)),
        grid=(t // tm,),
        in_specs=[
            row(D_MODEL),
            pl.BlockSpec((SUBLANES, D_MODEL), lambda i: (jnp.maximum(i * hb - 1, 0), 0)),
            pl.BlockSpec((SUBLANES, D_MODEL), lambda i: (jnp.minimum((i + 1) * hb, nh - 1), 0)),
            pl.BlockSpec((1, D_MODEL), const),
            pl.BlockSpec((D_MODEL, N_IN), const),
            pl.BlockSpec((1, CB), const),
            pl.BlockSpec((1, CB), const),
            pl.BlockSpec((CB, CB), const),
            pl.BlockSpec((3, N_HY_SLOTS * CB), const),
            pl.BlockSpec((1, N_HY_SLOTS * CB), const),
        ],
        out_specs=(row(z_w), row(CB), row(CB)),
        compiler_params=pltpu.CompilerParams(dimension_semantics=("arbitrary",)),
        name="inproj",
    )(x2, x2, x2, norm_g, w_in_bf, qgain, kgain, bd, conv_w, conv_b)


def _filter_kernel(zt_ref, t_ref, w1t_ref, b1_ref, fr1_ref, w2t_ref, b2_ref, fr2_ref, w3_ref, dl_ref, o_ref, *, l, tn):
    i = pl.program_id(0)
    h = jnp.sin(fr1_ref[...] * (_dot_hi(w1t_ref[...], zt_ref[...]) + b1_ref[...]))
    h = jnp.sin(fr2_ref[...] * (_dot_hi(w2t_ref[...], h) + b2_ref[...]))
    h = _dot_hi(h.T, w3_ref[...])
    h = h * jnp.exp(-t_ref[...] * dl_ref[...])
    row = i * tn + lax.broadcasted_iota(jnp.int32, h.shape, 0)
    o_ref[...] = jnp.where(row == l, 0.0, h).astype(o_ref.dtype)


def _filter(zpos2, w1p, b1, fr1, w2, b2, fr2, w3, absdeltas, l):
    n = 2 * l
    tn = min(512, l)
    nb_half = l // tn
    const = lambda i: (0, 0)
    half = lambda i: (0, i // nb_half)
    col = pl.BlockSpec((FILTER_HIDDEN, 1), const)
    return pl.pallas_call(
        functools.partial(_filter_kernel, l=l, tn=tn),
        out_shape=jax.ShapeDtypeStruct((n, HY_W), BF16),
        grid=(n // tn,),
        in_specs=[
            pl.BlockSpec((128, tn), lambda i: (0, i)),
            pl.BlockSpec((tn, 1), lambda i: (i, 0)),
            pl.BlockSpec((FILTER_HIDDEN, 128), const),
            col, col,
            pl.BlockSpec((FILTER_HIDDEN, FILTER_HIDDEN), const),
            col, col,
            pl.BlockSpec((FILTER_HIDDEN, HY_W), half),
            pl.BlockSpec((1, HY_W), half),
        ],
        out_specs=pl.BlockSpec((tn, HY_W), lambda i: (i, 0)),
        compiler_params=pltpu.CompilerParams(dimension_semantics=("arbitrary",)),
        name="hyena_filter",
    )(zpos2.T, zpos2[:, 0:1], w1p.T, b1.T, fr1.T, w2.T, b2.T, fr2.T, w3, absdeltas)


def _slab_kernel(w_ref, x_ref, o_ref):
    k, sub, c = x_ref.shape
    m = o_ref.shape[0]
    x = x_ref[...].astype(F32)
    halves = []
    for h in range(sub // SUBLANES):
        xh = x[:, h * SUBLANES:(h + 1) * SUBLANES, :].reshape(k * SUBLANES, c)
        halves.append(_dot(w_ref[...], xh.astype(BF16)).reshape(m, SUBLANES, c))
    o_ref[...] = jnp.concatenate(halves, axis=1).astype(o_ref.dtype)


def _slab_stage(wk, x, out_dtype):
    p, k, n2, c = x.shape
    m = wk.shape[0] // SUBLANES
    assert wk.shape == (m * SUBLANES, k * SUBLANES)
    return pl.pallas_call(
        _slab_kernel,
        out_shape=jax.ShapeDtypeStruct((p, m, n2, c), out_dtype),
        grid=(p, n2 // SLAB_ROWS),
        in_specs=[pl.BlockSpec(wk.shape, lambda pi, j: (0, 0)),
                  pl.BlockSpec((None, k, SLAB_ROWS, c), lambda pi, j: (pi, 0, j, 0))],
        out_specs=pl.BlockSpec((None, m, SLAB_ROWS, c), lambda pi, j: (pi, 0, j, 0)),
        compiler_params=pltpu.CompilerParams(dimension_semantics=("arbitrary", "arbitrary")),
        name="dft_outer",
    )(wk, x)


def _stacked_g(f2r, f2i, tr, ti):
    gr = f2r * tr - f2i * ti
    gi = f2r * ti + f2i * tr
    return gr, gi


def _stack(a, b, c, d):
    return jnp.concatenate([jnp.concatenate([a, b], axis=1), jnp.concatenate([c, d], axis=1)], axis=0)


def _mid_data_kernel(a_ref, kf_ref, f2r_ref, f2i_ref, twr_ref, twi_ref, o_ref):
    f2r = f2r_ref[...]
    f2i = f2i_ref[...]
    for j in range(MID_KB):
        gr, gi = _stacked_g(f2r, f2i, twr_ref[j:j + 1, :], twi_ref[j:j + 1, :])
        gs = _stack(gr, -gi, gi, gr).astype(BF16)
        a = jnp.concatenate([a_ref[0, j], a_ref[1, j]], axis=0)
        x = _dot(gs, a)
        xr, xi = x[:DFT_N2], x[DFT_N2:]
        kr, ki = kf_ref[0, j], kf_ref[1, j]
        y = jnp.concatenate([xr * kr - xi * ki, xr * ki + xi * kr], axis=0).astype(BF16)
        grt, git = gr.T, gi.T
        gh = _stack(grt, git, -git, grt).astype(BF16)
        bp = _dot(gh, y)
        o_ref[0, j] = bp[:DFT_N2].astype(o_ref.dtype)
        o_ref[1, j] = bp[DFT_N2:].astype(o_ref.dtype)


def _mid_filter_kernel(a_ref, f2r_ref, f2i_ref, twr_ref, twi_ref, o_ref):
    f2r = f2r_ref[...]
    f2i = f2i_ref[...]
    for j in range(MID_KB):
        gr, gi = _stacked_g(f2r, f2i, twr_ref[j:j + 1, :], twi_ref[j:j + 1, :])
        gs = _stack(gr, -gi, gi, gr).astype(BF16)
        a = jnp.concatenate([a_ref[0, j], a_ref[1, j]], axis=0)
        x = _dot(gs, a)
        o_ref[0, j] = x[:DFT_N2]
        o_ref[1, j] = x[DFT_N2:]


def _mid_specs(n1):
    sq = pl.BlockSpec((DFT_N2, DFT_N2), lambda k, p: (0, 0))
    tw = pl.BlockSpec((MID_KB, DFT_N2), lambda k, p: (k, 0))
    return sq, tw


def _mid_data(a5, kf, f2r, f2i, twr, twi):
    p, _, n1, n2, c = a5.shape
    sq, tw = _mid_specs(n1)
    blk = pl.BlockSpec((None, 2, MID_KB, n2, c), lambda k, pi: (pi, 0, k, 0, 0))
    return pl.pallas_call(
        _mid_data_kernel,
        out_shape=jax.ShapeDtypeStruct(a5.shape, BF16),
        grid=(n1 // MID_KB, p),
        in_specs=[blk, pl.BlockSpec((2, MID_KB, n2, c), lambda k, pi: (0, k, 0, 0)), sq, sq, tw, tw],
        out_specs=blk,
        compiler_params=pltpu.CompilerParams(dimension_semantics=("arbitrary", "arbitrary")),
        name="dft_mid",
    )(a5, kf, f2r, f2i, twr, twi)


def _mid_filter(a4, f2r, f2i, twr, twi):
    _, n1, n2, c = a4.shape
    sq, tw = _mid_specs(n1)
    blk = pl.BlockSpec((2, MID_KB, n2, c), lambda k, pi: (0, k, 0, 0))
    return pl.pallas_call(
        _mid_filter_kernel,
        out_shape=jax.ShapeDtypeStruct(a4.shape, F32),
        grid=(n1 // MID_KB, 1),
        in_specs=[blk, sq, sq, tw, tw],
        out_specs=blk,
        compiler_params=pltpu.CompilerParams(dimension_semantics=("arbitrary", "arbitrary")),
        name="dft_mid_filter",
    )(a4, f2r, f2i, twr, twi)


def _attn_kernel(q_ref, k_ref, v_ref, bias_ref, o_ref, kt_scr, s_scr, *, rows):
    rb = pl.program_id(1)
    lane = lax.broadcasted_iota(jnp.int32, (GRID_W, 128), 1)
    first = lane < HEAD_DIM
    ones = jnp.ones((KEY_CHUNK, 128), BF16)
    band = WIN_ROWS * GRID_W
    n_chunks = band // KEY_CHUNK
    n_keys = rows * GRID_W
    units_per_block = ATT_ROWS * GRID_W // KT_UNIT
    n_trips = ATT_ROWS // ATT_UNROLL
    assert units_per_block == n_trips
    pairs = range(N_HEADS // 2)

    def transpose_unit(m):
        lanes = pl.ds(pl.multiple_of(m * KT_UNIT, KT_UNIT), KT_UNIT)
        kt_scr[0, :, lanes] = k_ref[pl.ds(pl.multiple_of(m * KT_UNIT, KT_UNIT), KT_UNIT), :].T
        src = jnp.minimum(m * KT_UNIT + GRID_W, n_keys - KT_UNIT)
        kt_scr[1, :, lanes] = k_ref[pl.ds(pl.multiple_of(src, GRID_W), KT_UNIT), :].T

    @pl.when(rb == 0)
    def _():
        for j in range(units_per_block):
            transpose_unit(j)

    nxt = jnp.minimum(rb + 1, pl.num_programs(1) - 1)

    def trip(t, carry):
        transpose_unit(nxt * units_per_block + t)
        info = []
        for a in range(ATT_UNROLL):
            rl = t * ATT_UNROLL + a
            r = rb * ATT_ROWS + rl
            r0 = jnp.clip(r - WIN_ROWS // 2, 0, rows - WIN_ROWS)
            par = r0 % 2
            info.append((r - r0, par, (r0 - par) * GRID_W, r0 * GRID_W, pl.multiple_of(rl * GRID_W, GRID_W)))
        maxes = {}
        for a, (d, par, kt_start, start, qoff) in enumerate(info):
            for pr in pairs:
                cs = slice(pr * 128, (pr + 1) * 128)
                q2 = q_ref[pl.ds(qoff, GRID_W), cs]
                zero = jnp.zeros_like(q2)
                qbd = jnp.concatenate([jnp.where(first, q2, zero), jnp.where(first, zero, q2)], axis=0)
                macc = None
                for c in range(n_chunks):
                    kl = pl.ds(pl.multiple_of(kt_start + c * KEY_CHUNK, 128), KEY_CHUNK)
                    lanes = slice(c * KEY_CHUNK, (c + 1) * KEY_CHUNK)
                    row_pairs = range(c * KEY_CHUNK // KT_UNIT, (c + 1) * KEY_CHUNK // KT_UNIT)
                    bias = [bias_ref[2 * j + (WIN_ROWS - 1) - d, pr] for j in row_pairs]
                    s = _dot(qbd, kt_scr[par, cs, kl]) + jnp.concatenate(bias, axis=1)
                    s_scr[a, pr, :, lanes] = s
                    macc = s if macc is None else jnp.maximum(macc, s)
                maxes[a, pr] = jnp.max(macc, axis=-1, keepdims=True)
        for a, (d, par, kt_start, start, qoff) in enumerate(info):
            for pr in pairs:
                cs = slice(pr * 128, (pr + 1) * 128)
                acc = jnp.zeros((2 * GRID_W, 256), F32)
                for c in range(n_chunks):
                    ks = pl.ds(pl.multiple_of(start + c * KEY_CHUNK, GRID_W), KEY_CHUNK)
                    lanes = slice(c * KEY_CHUNK, (c + 1) * KEY_CHUNK)
                    p = jnp.exp2(s_scr[a, pr, :, lanes] - maxes[a, pr]).astype(BF16)
                    acc = acc + _dot(p, jnp.concatenate([v_ref[ks, cs], ones], axis=1))
                o = acc[:, :128] / acc[:, 128:]
                o_ref[pl.ds(qoff, GRID_W), cs] = jnp.where(first, o[:GRID_W], o[GRID_W:]).astype(o_ref.dtype)
        return carry

    lax.fori_loop(0, n_trips, trip, 0)


def _attention(z3, bias_tab):
    b, l, _ = z3.shape
    rows = l // GRID_W
    tq = ATT_ROWS * GRID_W
    return pl.pallas_call(
        functools.partial(_attn_kernel, rows=rows),
        out_shape=jax.ShapeDtypeStruct((b, l, ATT_W), BF16),
        grid=(b, rows // ATT_ROWS),
        in_specs=[
            pl.BlockSpec((None, tq, CB), lambda bi, i: (bi, i, Z_Q)),
            pl.BlockSpec((None, l, CB), lambda bi, i: (bi, 0, Z_K), pipeline_mode=pl.Buffered(1)),
            pl.BlockSpec((None, l, CB), lambda bi, i: (bi, 0, Z_V)),
            pl.BlockSpec(bias_tab.shape, lambda bi, i: (0, 0, 0, 0), pipeline_mode=pl.Buffered(1)),
        ],
        out_specs=pl.BlockSpec((None, tq, ATT_W), lambda bi, i: (bi, i, 0)),
        scratch_shapes=[pltpu.VMEM((2, ATT_W, l), BF16),
                        pltpu.VMEM((ATT_UNROLL, N_HEADS // 2, 2 * GRID_W, WIN_ROWS * GRID_W), F32)],
        compiler_params=pltpu.CompilerParams(dimension_semantics=("arbitrary", "arbitrary")),
        name="nbr_attention",
    )(z3, z3, z3, bias_tab)


def _outproj_kernel(x_ref, yl_ref, u_ref, x0_ref, gh_ref, at_ref, ga_ref, hb_ref, onh_ref, ona_ref, w_ref, o_ref):
    yh = x0_ref[...].astype(F32) * (yl_ref[...] + u_ref[...].astype(F32) * hb_ref[...])
    yh = yh * lax.rsqrt(jnp.mean(yh * yh, axis=-1, keepdims=True) + EPS) * onh_ref[...]
    yh = yh * jax.nn.silu(gh_ref[...].astype(F32))
    ya = at_ref[...].astype(F32)
    ya = ya * lax.rsqrt(jnp.mean(ya * ya, axis=-1, keepdims=True) + EPS) * ona_ref[...]
    ya = ya * jax.nn.silu(ga_ref[...].astype(F32))
    acc = _dot(yh.astype(BF16), w_ref[:HY_W, :]) + _dot(ya.astype(BF16), w_ref[HY_W:, :])
    o_ref[...] = x_ref[...] + acc


def _outproj(x2, ylong, u, x0c, z2, att, hy_bias, on_hy, on_att, w_out_bf):
    t = x2.shape[0]
    tm = TOKEN_TILE
    row = lambda c: pl.BlockSpec((tm, CB), lambda i, c=c: (i, c))
    vec = pl.BlockSpec((1, CB), lambda i: (0, 0))
    return pl.pallas_call(
        _outproj_kernel,
        out_shape=jax.ShapeDtypeStruct((t, D_MODEL), F32),
        grid=(t // tm,),
        in_specs=[pl.BlockSpec((tm, D_MODEL), lambda i: (i, 0)), row(0), row(0), row(0), row(Z_GATE_HY), row(0),
                  row(Z_GATE_ATT),
                  vec, vec, vec, pl.BlockSpec((HY_W + ATT_W, D_MODEL), lambda i: (0, 0))],
        out_specs=pl.BlockSpec((tm, D_MODEL), lambda i: (i, 0)),
        compiler_params=pltpu.CompilerParams(dimension_semantics=("arbitrary",)),
        name="outproj",
    )(x2, ylong, u, x0c, z2, att, z2, hy_bias, on_hy, on_att, w_out_bf)


def _dft_tables(l):
    n = 2 * l
    n2 = DFT_N2
    n1 = n // n2
    h = n1 // 2
    k1 = np.arange(n1)
    ang1 = -2.0 * np.pi * ((k1[:, None] * k1[None, :]) % n1) / n1
    f1r, f1i = np.cos(ang1), np.sin(ang1)
    w_data = np.block([[f1r[:, :h], -f1i[:, :h]], [f1i[:, :h], f1r[:, :h]]])
    w_filt = np.concatenate([f1r, f1i], axis=0)
    w_inv = np.block([[f1r[:h], f1i[:h]], [-f1i[:h], f1r[:h]]]) / n
    k2 = np.arange(n2)
    ang2 = -2.0 * np.pi * ((k2[:, None] * k2[None, :]) % n2) / n2
    angt = -2.0 * np.pi * (k1[:, None] * k2[None, :]) / n
    f = lambda a: jnp.asarray(a, dtype=F32)
    eye = jnp.eye(SUBLANES, dtype=F32)

    def kron_eye(w):
        w = f(w)
        return (w[:, None, :, None] * eye[None, :, None, :]).reshape(w.shape[0] * SUBLANES, w.shape[1] * SUBLANES)

    return dict(n1=n1, w_data=kron_eye(w_data).astype(BF16), w_filt=kron_eye(w_filt).astype(BF16),
                w_inv=kron_eye(w_inv).astype(BF16),
                f2r=f(np.cos(ang2)), f2i=f(np.sin(ang2)), twr=f(np.cos(angt)), twi=f(np.sin(angt)))


def _positional_features(l):
    t01 = jnp.linspace(0.0, 1.0, l, dtype=F32)[:, None]
    w = 2.0 * math.pi * jnp.arange(l, dtype=F32)[:, None] / l
    f = jnp.linspace(1e-4, POS_BANDS - 1, POS_BANDS, dtype=F32)[None, :]
    z = jnp.concatenate([t01, jnp.cos(f * w), -jnp.sin(f * w)], axis=-1)
    z2 = jnp.concatenate([z, jnp.zeros((1, z.shape[1]), F32), z[::-1][:l - 1]], axis=0)
    return jnp.pad(z2, ((0, 0), (0, 128 - z2.shape[1])))


def _abs_deltas():
    min_decay = math.log(1e-2) / 1.5
    max_decay = math.log(1e-2) / 0.3
    return jnp.abs(jnp.linspace(min_decay, max_decay, 2 * HY_W, dtype=F32))[None, :]


def _bias_table(rpb):
    cols = np.arange(GRID_W)
    cstart = np.clip(cols - WIN_COLS // 2, 0, GRID_W - WIN_COLS)
    kc = np.arange(GRID_W)
    valid = (kc[None, :] >= cstart[:, None]) & (kc[None, :] < cstart[:, None] + WIN_COLS)
    rel = np.clip(kc[None, :] - cols[:, None] + (WIN_COLS - 1), 0, 2 * WIN_COLS - 2)
    t = jnp.where(valid[None, None], rpb.astype(F32)[:, :, rel] * LOG2E, NEG_INF)
    t = jnp.stack([t[:, :-1], t[:, 1:]], axis=3)
    t = t.reshape(N_HEADS, 2 * WIN_ROWS - 2, GRID_W, 2 * GRID_W).transpose(1, 0, 2, 3)
    return t.reshape(2 * WIN_ROWS - 2, N_HEADS // 2, 2 * GRID_W, 2 * GRID_W)


def _filter_spectrum(l, tabs, zpos2, absd, f_w1, f_b1, f_fr1, f_w2, f_b2, f_fr2, f_w3):
    n1 = tabs["n1"]
    w1p = jnp.pad(f_w1, ((0, 128 - f_w1.shape[0]), (0, 0)))
    kfilt = _filter(zpos2, w1p, f_b1[None], f_fr1[None], f_w2, f_b2[None], f_fr2[None], f_w3, absd, l)
    a = _slab_stage(tabs["w_filt"], kfilt.reshape(1, n1, DFT_N2, HY_W), BF16)
    return _mid_filter(a.reshape(2, n1, DFT_N2, HY_W), tabs["f2r"], tabs["f2i"], tabs["twr"], tabs["twi"])


def _layer(x, tabs, kf, bias_tab, bd, norm_g, w_in, conv_w, conv_b, hy_bias, qn_g, kn_g, on_hy, on_att, w_out):
    b, l, _ = x.shape
    t = b * l
    n1 = tabs["n1"]
    x2 = x.reshape(t, D_MODEL)
    qgain = jnp.tile(qn_g, N_HEADS)[None] * (HEAD_DIM ** -0.5 * LOG2E)
    kgain = jnp.tile(kn_g, N_HEADS)[None]
    z2, u, x0c = _inproj(x2, l, norm_g[None], w_in.astype(BF16), qgain, kgain, bd, conv_w, conv_b[None])
    z3 = z2.reshape(b, l, z2.shape[1])
    a = _slab_stage(tabs["w_data"], u.reshape(b // 2, n1, DFT_N2, HY_W), BF16)
    bp = _mid_data(a.reshape(b // 2, 2, n1, DFT_N2, HY_W), kf, tabs["f2r"], tabs["f2i"], tabs["twr"], tabs["twi"])
    ylong = _slab_stage(tabs["w_inv"], bp.reshape(b // 2, 2 * n1, DFT_N2, HY_W), F32)
    att = _attention(z3, bias_tab)
    out = _outproj(x2, ylong.reshape(t, HY_W), u.reshape(t, HY_W), x0c.reshape(t, HY_W), z2,
                   att.reshape(t, ATT_W), hy_bias[None], on_hy[None], on_att[None], w_out.astype(BF16))
    return out.reshape(b, l, D_MODEL)


def _trunk(x, norm_g, w_in, conv_w, conv_b, f_w1, f_b1, f_fr1, f_w2, f_b2, f_fr2, f_w3,
           hy_bias, qn_g, kn_g, rpb, on_hy, on_att, w_out):
    b, l, _ = x.shape
    assert b % 2 == 0 and l % (GRID_W * ATT_ROWS) == 0 and l // GRID_W >= WIN_ROWS
    assert (2 * l) % (DFT_N2 * MID_KB) == 0 and l % TOKEN_TILE == 0
    tabs = _dft_tables(l)
    zpos2 = _positional_features(l)
    absd = _abs_deltas()
    head = np.arange(CB) // HEAD_DIM
    bd = jnp.asarray((head[:, None] == head[None, :]).astype(np.float32) / HEAD_DIM, dtype=BF16)
    for i in range(norm_g.shape[0]):
        kf = _filter_spectrum(l, tabs, zpos2, absd, f_w1[i], f_b1[i], f_fr1[i], f_w2[i], f_b2[i], f_fr2[i], f_w3[i])
        x = _layer(x, tabs, kf, _bias_table(rpb[i]), bd, norm_g[i], w_in[i], conv_w[i], conv_b[i], hy_bias[i],
                   qn_g[i], kn_g[i], on_hy[i], on_att[i], w_out[i])
    return x


def kernel(x_prompt, x_sample, norm_g, w_in, conv_w, conv_b, f_w1, f_b1, f_fr1, f_w2, f_b2, f_fr2, f_w3,
           hy_bias, qn_g, kn_g, rpb, on_hy, on_att, w_out):
    params = (norm_g, w_in, conv_w, conv_b, f_w1, f_b1, f_fr1, f_w2, f_b2, f_fr2, f_w3,
              hy_bias, qn_g, kn_g, rpb, on_hy, on_att, w_out)
    return (_trunk(x_prompt, *params), _trunk(x_sample, *params))
```

```python
import functools
import math

import numpy as np
import jax
import jax.numpy as jnp
from jax import lax
from jax.experimental import pallas as pl
from jax.experimental.pallas import tpu as pltpu

F32 = jnp.float32
BF16 = jnp.bfloat16

D_MODEL = 1024
GRID_W = 64
HY_W = 512
ATT_W = 512
HEAD_DIM = 64
N_HEADS = 8
WIN_ROWS = 8
WIN_COLS = 16
POS_BANDS = 16
FILTER_HIDDEN = 64
N_IN = 4 * HY_W + 4 * ATT_W
EPS = 1e-6
NEG_INF = -1e30

CB = 512
DFT_N2 = 256
TOKEN_TILE = 512
N_HY_SLOTS = 3
Z_GATE_HY, Z_Q, Z_K, Z_V, Z_GATE_ATT = range(5)
ATT_ROWS = 8
KEY_CHUNK = 256
KT_UNIT = 128
ATT_UNROLL = 2
LOG2E = math.log2(math.e)
MID_KB = 8
MID_COLS = 256
SUBLANES = 8
SLAB_ROWS = 16


def _dot(a, b):
    return jnp.dot(a, b, preferred_element_type=F32)


def _dot_hi(a, b):
    return jnp.dot(a, b, preferred_element_type=F32, precision=lax.Precision.HIGHEST)


def _split(x):
    hi = x.astype(BF16)
    lo = (x - hi.astype(F32)).astype(BF16)
    return hi, lo


def _dot3(a, b):
    ah, al = _split(a)
    bh, bl = _split(b)
    return _dot(ah, bh) + (_dot(ah, bl) + _dot(al, bh))


def _inproj_kernel(x_ref, xp_ref, xn_ref, g_ref, w_ref, qg_ref, kg_ref, bd_ref, cw_ref, cb_ref,
                   z_ref, u_ref, x0_ref, *, tiles_per_seq):
    pos = pl.program_id(0) % tiles_per_seq
    tm = x_ref.shape[0]

    def norm(x):
        ms = jnp.mean(x * x, axis=-1, keepdims=True)
        return (x * lax.rsqrt(ms + EPS) * g_ref[...]).astype(BF16)

    h = norm(x_ref[...])
    hp = norm(xp_ref[...])
    hn = norm(xn_ref[...])
    rows = lax.broadcasted_iota(jnp.int32, (tm, CB), 0)
    conv = []
    for j in range(N_HY_SLOTS):
        cols = slice(j * CB, (j + 1) * CB)
        cur = _dot(h, w_ref[:, cols])
        prev_row = jnp.where(pos > 0, _dot(hp, w_ref[:, cols])[SUBLANES - 1:SUBLANES, :], 0.0)
        next_row = jnp.where(pos < tiles_per_seq - 1, _dot(hn, w_ref[:, cols])[0:1, :], 0.0)
        dn = jnp.where(rows == 0, prev_row, pltpu.roll(cur, 1, 0))
        up = jnp.where(rows == tm - 1, next_row, pltpu.roll(cur, tm - 1, 0))
        cw = cw_ref[:, cols]
        conv.append(cw[0:1, :] * dn + cw[1:2, :] * cur + cw[2:3, :] * up + cb_ref[:, cols])
    x0_ref[...] = conv[0].astype(x0_ref.dtype)
    u_ref[...] = (conv[1] * conv[2]).astype(u_ref.dtype)
    for j in range(N_HY_SLOTS, N_IN // CB):
        zj = _dot(h, w_ref[:, j * CB:(j + 1) * CB])
        if j in (4, 5):
            hm = _dot((zj * zj).astype(BF16), bd_ref[...])
            gain = qg_ref[...] if j == 4 else kg_ref[...]
            zj = zj * lax.rsqrt(hm + EPS) * gain
        z_ref[:, (j - N_HY_SLOTS) * CB:(j - N_HY_SLOTS + 1) * CB] = zj.astype(z_ref.dtype)


def _inproj(x2, seq_len, norm_g, w_in_bf, qgain, kgain, bd, conv_w, conv_b):
    t = x2.shape[0]
    tm = TOKEN_TILE
    hb = tm // SUBLANES
    nh = t // SUBLANES
    const = lambda i: (0, 0)
    row = lambda w: pl.BlockSpec((tm, w), lambda i: (i, 0))
    z_w = N_IN - N_HY_SLOTS * CB
    return pl.pallas_call(
        functools.partial(_inproj_kernel, tiles_per_seq=seq_len // tm),
        out_shape=(jax.ShapeDtypeStruct((t, z_w), BF16), jax.ShapeDtypeStruct((t, CB), BF16),
                   jax.ShapeDtypeStruct((t, CB), BF16)),
        grid=(t // tm,),
        in_specs=[
            row(D_MODEL),
            pl.BlockSpec((SUBLANES, D_MODEL), lambda i: (jnp.maximum(i * hb - 1, 0), 0)),
            pl.BlockSpec((SUBLANES, D_MODEL), lambda i: (jnp.minimum((i + 1) * hb, nh - 1), 0)),
            pl.BlockSpec((1, D_MODEL), const),
            pl.BlockSpec((D_MODEL, N_IN), const),
            pl.BlockSpec((1, CB), const),
            pl.BlockSpec((1, CB), const),
            pl.BlockSpec((CB, CB), const),
            pl.BlockSpec((3, N_HY_SLOTS * CB), const),
            pl.BlockSpec((1, N_HY_SLOTS * CB), const),
        ],
        out_specs=(row(z_w), row(CB), row(CB)),
        compiler_params=pltpu.CompilerParams(dimension_semantics=("arbitrary",)),
        name="inproj",
    )(x2, x2, x2, norm_g, w_in_bf, qgain, kgain, bd, conv_w, conv_b)


def _filter_kernel(zt_ref, t_ref, w1t_ref, b1_ref, fr1_ref, w2t_ref, b2_ref, fr2_ref, w3_ref, dl_ref, o_ref, *, l, tn):
    i = pl.program_id(0)
    h = jnp.sin(fr1_ref[...] * (_dot_hi(w1t_ref[...], zt_ref[...]) + b1_ref[...]))
    h = jnp.sin(fr2_ref[...] * (_dot_hi(w2t_ref[...], h) + b2_ref[...]))
    h = _dot_hi(h.T, w3_ref[...])
    h = h * jnp.exp(-t_ref[...] * dl_ref[...])
    row = i * tn + lax.broadcasted_iota(jnp.int32, h.shape, 0)
    o_ref[...] = jnp.where(row == l, 0.0, h).astype(o_ref.dtype)


def _filter(zpos2, w1p, b1, fr1, w2, b2, fr2, w3, absdeltas, l):
    n = 2 * l
    tn = min(512, l)
    nb_half = l // tn
    const = lambda i: (0, 0)
    half = lambda i: (0, i // nb_half)
    col = pl.BlockSpec((FILTER_HIDDEN, 1), const)
    return pl.pallas_call(
        functools.partial(_filter_kernel, l=l, tn=tn),
        out_shape=jax.ShapeDtypeStruct((n, HY_W), BF16),
        grid=(n // tn,),
        in_specs=[
            pl.BlockSpec((128, tn), lambda i: (0, i)),
            pl.BlockSpec((tn, 1), lambda i: (i, 0)),
            pl.BlockSpec((FILTER_HIDDEN, 128), const),
            col, col,
            pl.BlockSpec((FILTER_HIDDEN, FILTER_HIDDEN), const),
            col, col,
            pl.BlockSpec((FILTER_HIDDEN, HY_W), half),
            pl.BlockSpec((1, HY_W), half),
        ],
        out_specs=pl.BlockSpec((tn, HY_W), lambda i: (i, 0)),
        compiler_params=pltpu.CompilerParams(dimension_semantics=("arbitrary",)),
        name="hyena_filter",
    )(zpos2.T, zpos2[:, 0:1], w1p.T, b1.T, fr1.T, w2.T, b2.T, fr2.T, w3, absdeltas)


def _slab_kernel(w_ref, x_ref, o_ref, wk_scr):
    k, sub, c = x_ref.shape
    m = o_ref.shape[0]

    @pl.when((pl.program_id(0) == 0) & (pl.program_id(1) == 0))
    def _():
        m8, k8 = wk_scr.shape
        onehot = lambda cond: jnp.where(cond, 1.0, 0.0).astype(BF16)
        col_exp = onehot(lax.broadcasted_iota(jnp.int32, (k, k8), 1) // SUBLANES
                         == lax.broadcasted_iota(jnp.int32, (k, k8), 0))
        wcols = _dot(w_ref[...].astype(BF16), col_exp).astype(BF16)
        rb = min(m8, 256)
        for r in range(m8 // rb):
            rows = r * rb + lax.broadcasted_iota(jnp.int32, (rb, m), 0)
            row_exp = onehot(rows // SUBLANES == lax.broadcasted_iota(jnp.int32, (rb, m), 1))
            blk = _dot(row_exp, wcols)
            same = (lax.broadcasted_iota(jnp.int32, (rb, k8), 0) % SUBLANES
                    == lax.broadcasted_iota(jnp.int32, (rb, k8), 1) % SUBLANES)
            wk_scr[r * rb:(r + 1) * rb, :] = jnp.where(same, blk, 0.0).astype(BF16)

    x = x_ref[...].astype(F32)
    halves = []
    for h in range(sub // SUBLANES):
        xh = x[:, h * SUBLANES:(h + 1) * SUBLANES, :].reshape(k * SUBLANES, c)
        halves.append(_dot(wk_scr[...], xh.astype(BF16)).reshape(m, SUBLANES, c))
    o_ref[...] = jnp.concatenate(halves, axis=1).astype(o_ref.dtype)


def _slab_stage(w, x, out_dtype):
    p, k, n2, c = x.shape
    m = w.shape[0]
    assert w.shape == (m, k)
    return pl.pallas_call(
        _slab_kernel,
        out_shape=jax.ShapeDtypeStruct((p, m, n2, c), out_dtype),
        grid=(p, n2 // SLAB_ROWS),
        in_specs=[pl.BlockSpec((m, k), lambda pi, j: (0, 0)),
                  pl.BlockSpec((None, k, SLAB_ROWS, c), lambda pi, j: (pi, 0, j, 0))],
        out_specs=pl.BlockSpec((None, m, SLAB_ROWS, c), lambda pi, j: (pi, 0, j, 0)),
        scratch_shapes=[pltpu.VMEM((m * SUBLANES, k * SUBLANES), BF16)],
        compiler_params=pltpu.CompilerParams(dimension_semantics=("arbitrary", "arbitrary")),
        name="dft_outer",
    )(w, x)


def _stacked_g(f2r, f2i, tr, ti):
    gr = f2r * tr - f2i * ti
    gi = f2r * ti + f2i * tr
    return gr, gi


def _stack(a, b, c, d):
    return jnp.concatenate([jnp.concatenate([a, b], axis=1), jnp.concatenate([c, d], axis=1)], axis=0)


def _mid_data_kernel(a_ref, kf_ref, f2r_ref, f2i_ref, twr_ref, twi_ref, o_ref, gs_scr, gh_scr):
    @pl.when(pl.program_id(1) == 0)
    def _():
        f2r = f2r_ref[...]
        f2i = f2i_ref[...]
        for j in range(MID_KB):
            gr, gi = _stacked_g(f2r, f2i, twr_ref[j:j + 1, :], twi_ref[j:j + 1, :])
            gs_scr[j] = _stack(gr, -gi, gi, gr).astype(BF16)
            grt, git = gr.T, gi.T
            gh_scr[j] = _stack(grt, git, -git, grt).astype(BF16)

    c = a_ref.shape[-1]
    for j in range(MID_KB):
        for cb in range(c // MID_COLS):
            cols = slice(cb * MID_COLS, (cb + 1) * MID_COLS)
            a = jnp.concatenate([a_ref[0, j, :, cols], a_ref[1, j, :, cols]], axis=0)
            x = _dot(gs_scr[j], a)
            xr, xi = x[:DFT_N2], x[DFT_N2:]
            kr, ki = kf_ref[0, j, :, cols].astype(F32), kf_ref[1, j, :, cols].astype(F32)
            y = jnp.concatenate([xr * kr - xi * ki, xr * ki + xi * kr], axis=0).astype(BF16)
            bp = _dot(gh_scr[j], y)
            o_ref[0, j, :, cols] = bp[:DFT_N2].astype(o_ref.dtype)
            o_ref[1, j, :, cols] = bp[DFT_N2:].astype(o_ref.dtype)


def _mid_filter_kernel(a_ref, f2r_ref, f2i_ref, twr_ref, twi_ref, o_ref):
    f2r = f2r_ref[...]
    f2i = f2i_ref[...]
    for j in range(MID_KB):
        gr, gi = _stacked_g(f2r, f2i, twr_ref[j:j + 1, :], twi_ref[j:j + 1, :])
        gs = _stack(gr, -gi, gi, gr).astype(BF16)
        a = jnp.concatenate([a_ref[0, j], a_ref[1, j]], axis=0)
        x = _dot(gs, a)
        o_ref[0, j] = x[:DFT_N2].astype(o_ref.dtype)
        o_ref[1, j] = x[DFT_N2:].astype(o_ref.dtype)


def _mid_specs(n1):
    sq = pl.BlockSpec((DFT_N2, DFT_N2), lambda k, p: (0, 0))
    tw = pl.BlockSpec((MID_KB, DFT_N2), lambda k, p: (k, 0))
    return sq, tw


def _mid_data(a5, kf, f2r, f2i, twr, twi):
    p, _, n1, n2, c = a5.shape
    sq, tw = _mid_specs(n1)
    blk = pl.BlockSpec((None, 2, MID_KB, n2, c), lambda k, pi: (pi, 0, k, 0, 0))
    return pl.pallas_call(
        _mid_data_kernel,
        out_shape=jax.ShapeDtypeStruct(a5.shape, BF16),
        grid=(n1 // MID_KB, p),
        in_specs=[blk, pl.BlockSpec((2, MID_KB, n2, c), lambda k, pi: (0, k, 0, 0)), sq, sq, tw, tw],
        out_specs=blk,
        scratch_shapes=[pltpu.VMEM((MID_KB, 2 * n2, 2 * n2), BF16), pltpu.VMEM((MID_KB, 2 * n2, 2 * n2), BF16)],
        compiler_params=pltpu.CompilerParams(dimension_semantics=("arbitrary", "arbitrary")),
        name="dft_mid",
    )(a5, kf, f2r, f2i, twr, twi)


def _mid_filter(a4, f2r, f2i, twr, twi):
    _, n1, n2, c = a4.shape
    sq, tw = _mid_specs(n1)
    blk = pl.BlockSpec((2, MID_KB, n2, c), lambda k, pi: (0, k, 0, 0))
    return pl.pallas_call(
        _mid_filter_kernel,
        out_shape=jax.ShapeDtypeStruct(a4.shape, BF16),
        grid=(n1 // MID_KB, 1),
        in_specs=[blk, sq, sq, tw, tw],
        out_specs=blk,
        compiler_params=pltpu.CompilerParams(dimension_semantics=("arbitrary", "arbitrary")),
        name="dft_mid_filter",
    )(a4, f2r, f2i, twr, twi)


def _attn_kernel(q_ref, k_ref, v_ref, bias_ref, o_ref, kt_scr, s_scr, *, rows):
    rb = pl.program_id(1)
    lane = lax.broadcasted_iota(jnp.int32, (GRID_W, 128), 1)
    first = lane < HEAD_DIM
    ones = jnp.ones((KEY_CHUNK, 128), BF16)
    band = WIN_ROWS * GRID_W
    n_chunks = band // KEY_CHUNK
    n_keys = rows * GRID_W
    units_per_block = ATT_ROWS * GRID_W // KT_UNIT
    n_trips = ATT_ROWS // ATT_UNROLL
    assert units_per_block == n_trips
    pairs = range(N_HEADS // 2)

    def transpose_unit(m):
        lanes = pl.ds(pl.multiple_of(m * KT_UNIT, KT_UNIT), KT_UNIT)
        kt_scr[0, :, lanes] = k_ref[pl.ds(pl.multiple_of(m * KT_UNIT, KT_UNIT), KT_UNIT), :].T
        src = jnp.minimum(m * KT_UNIT + GRID_W, n_keys - KT_UNIT)
        kt_scr[1, :, lanes] = k_ref[pl.ds(pl.multiple_of(src, GRID_W), KT_UNIT), :].T

    @pl.when(rb == 0)
    def _():
        for j in range(units_per_block):
            transpose_unit(j)

    nxt = jnp.minimum(rb + 1, pl.num_programs(1) - 1)

    def trip(t, carry):
        transpose_unit(nxt * units_per_block + t)
        info = []
        for a in range(ATT_UNROLL):
            rl = t * ATT_UNROLL + a
            r = rb * ATT_ROWS + rl
            r0 = jnp.clip(r - WIN_ROWS // 2, 0, rows - WIN_ROWS)
            par = r0 % 2
            info.append((r - r0, par, (r0 - par) * GRID_W, r0 * GRID_W, pl.multiple_of(rl * GRID_W, GRID_W)))
        maxes = {}
        for a, (d, par, kt_start, start, qoff) in enumerate(info):
            for pr in pairs:
                cs = slice(pr * 128, (pr + 1) * 128)
                q2 = q_ref[pl.ds(qoff, GRID_W), cs]
                zero = jnp.zeros_like(q2)
                qbd = jnp.concatenate([jnp.where(first, q2, zero), jnp.where(first, zero, q2)], axis=0)
                macc = None
                for c in range(n_chunks):
                    kl = pl.ds(pl.multiple_of(kt_start + c * KEY_CHUNK, 128), KEY_CHUNK)
                    lanes = slice(c * KEY_CHUNK, (c + 1) * KEY_CHUNK)
                    row_pairs = range(c * KEY_CHUNK // KT_UNIT, (c + 1) * KEY_CHUNK // KT_UNIT)
                    bias = [bias_ref[2 * j + (WIN_ROWS - 1) - d, pr] for j in row_pairs]
                    s = _dot(qbd, kt_scr[par, cs, kl]) + jnp.concatenate(bias, axis=1)
                    s_scr[a, pr, :, lanes] = s
                    macc = s if macc is None else jnp.maximum(macc, s)
                maxes[a, pr] = jnp.max(macc, axis=-1, keepdims=True)
        for a, (d, par, kt_start, start, qoff) in enumerate(info):
            for pr in pairs:
                cs = slice(pr * 128, (pr + 1) * 128)
                acc = jnp.zeros((2 * GRID_W, 256), F32)
                for c in range(n_chunks):
                    ks = pl.ds(pl.multiple_of(start + c * KEY_CHUNK, GRID_W), KEY_CHUNK)
                    lanes = slice(c * KEY_CHUNK, (c + 1) * KEY_CHUNK)
                    p = jnp.exp2(s_scr[a, pr, :, lanes] - maxes[a, pr]).astype(BF16)
                    acc = acc + _dot(p, jnp.concatenate([v_ref[ks, cs], ones], axis=1))
                o = acc[:, :128] / acc[:, 128:]
                o_ref[pl.ds(qoff, GRID_W), cs] = jnp.where(first, o[:GRID_W], o[GRID_W:]).astype(o_ref.dtype)
        return carry

    lax.fori_loop(0, n_trips, trip, 0)


def _attention(z3, bias_tab):
    b, l, _ = z3.shape
    rows = l // GRID_W
    tq = ATT_ROWS * GRID_W
    return pl.pallas_call(
        functools.partial(_attn_kernel, rows=rows),
        out_shape=jax.ShapeDtypeStruct((b, l, ATT_W), BF16),
        grid=(b, rows // ATT_ROWS),
        in_specs=[
            pl.BlockSpec((None, tq, CB), lambda bi, i: (bi, i, Z_Q)),
            pl.BlockSpec((None, l, CB), lambda bi, i: (bi, 0, Z_K), pipeline_mode=pl.Buffered(1)),
            pl.BlockSpec((None, l, CB), lambda bi, i: (bi, 0, Z_V)),
            pl.BlockSpec(bias_tab.shape, lambda bi, i: (0, 0, 0, 0), pipeline_mode=pl.Buffered(1)),
        ],
        out_specs=pl.BlockSpec((None, tq, ATT_W), lambda bi, i: (bi, i, 0)),
        scratch_shapes=[pltpu.VMEM((2, ATT_W, l), BF16),
                        pltpu.VMEM((ATT_UNROLL, N_HEADS // 2, 2 * GRID_W, WIN_ROWS * GRID_W), F32)],
        compiler_params=pltpu.CompilerParams(dimension_semantics=("arbitrary", "arbitrary")),
        name="nbr_attention",
    )(z3, z3, z3, bias_tab)


def _outproj_kernel(x_ref, yl_ref, u_ref, x0_ref, gh_ref, at_ref, ga_ref, hb_ref, onh_ref, ona_ref, w_ref, o_ref):
    yh = x0_ref[...].astype(F32) * (yl_ref[...] + u_ref[...].astype(F32) * hb_ref[...])
    yh = yh * lax.rsqrt(jnp.mean(yh * yh, axis=-1, keepdims=True) + EPS) * onh_ref[...]
    yh = yh * jax.nn.silu(gh_ref[...].astype(F32))
    ya = at_ref[...].astype(F32)
    ya = ya * lax.rsqrt(jnp.mean(ya * ya, axis=-1, keepdims=True) + EPS) * ona_ref[...]
    ya = ya * jax.nn.silu(ga_ref[...].astype(F32))
    acc = _dot(yh.astype(BF16), w_ref[:HY_W, :]) + _dot(ya.astype(BF16), w_ref[HY_W:, :])
    o_ref[...] = x_ref[...] + acc


def _outproj(x2, ylong, u, x0c, z2, att, hy_bias, on_hy, on_att, w_out_bf):
    t = x2.shape[0]
    tm = TOKEN_TILE
    row = lambda c: pl.BlockSpec((tm, CB), lambda i, c=c: (i, c))
    vec = pl.BlockSpec((1, CB), lambda i: (0, 0))
    return pl.pallas_call(
        _outproj_kernel,
        out_shape=jax.ShapeDtypeStruct((t, D_MODEL), F32),
        grid=(t // tm,),
        in_specs=[pl.BlockSpec((tm, D_MODEL), lambda i: (i, 0)), row(0), row(0), row(0), row(Z_GATE_HY), row(0),
                  row(Z_GATE_ATT),
                  vec, vec, vec, pl.BlockSpec((HY_W + ATT_W, D_MODEL), lambda i: (0, 0))],
        out_specs=pl.BlockSpec((tm, D_MODEL), lambda i: (i, 0)),
        compiler_params=pltpu.CompilerParams(dimension_semantics=("arbitrary",)),
        name="outproj",
    )(x2, ylong, u, x0c, z2, att, z2, hy_bias, on_hy, on_att, w_out_bf)


def _dft_tables(l):
    n = 2 * l
    n2 = DFT_N2
    n1 = n // n2
    h = n1 // 2
    k1 = np.arange(n1)
    ang1 = -2.0 * np.pi * ((k1[:, None] * k1[None, :]) % n1) / n1
    f1r, f1i = np.cos(ang1), np.sin(ang1)
    w_data = np.block([[f1r[:, :h], -f1i[:, :h]], [f1i[:, :h], f1r[:, :h]]])
    w_filt = np.concatenate([f1r, f1i], axis=0)
    w_inv = np.block([[f1r[:h], f1i[:h]], [-f1i[:h], f1r[:h]]]) / n
    k2 = np.arange(n2)
    ang2 = -2.0 * np.pi * ((k2[:, None] * k2[None, :]) % n2) / n2
    angt = -2.0 * np.pi * (k1[:, None] * k2[None, :]) / n
    f = lambda a: jnp.asarray(a, dtype=F32)
    return dict(n1=n1, w_data=f(w_data), w_filt=f(w_filt), w_inv=f(w_inv),
                f2r=f(np.cos(ang2)), f2i=f(np.sin(ang2)), twr=f(np.cos(angt)), twi=f(np.sin(angt)))


def _positional_features(l):
    t01 = jnp.linspace(0.0, 1.0, l, dtype=F32)[:, None]
    w = 2.0 * math.pi * jnp.arange(l, dtype=F32)[:, None] / l
    f = jnp.linspace(1e-4, POS_BANDS - 1, POS_BANDS, dtype=F32)[None, :]
    z = jnp.concatenate([t01, jnp.cos(f * w), -jnp.sin(f * w)], axis=-1)
    z2 = jnp.concatenate([z, jnp.zeros((1, z.shape[1]), F32), z[::-1][:l - 1]], axis=0)
    return jnp.pad(z2, ((0, 0), (0, 128 - z2.shape[1])))


def _abs_deltas():
    min_decay = math.log(1e-2) / 1.5
    max_decay = math.log(1e-2) / 0.3
    return jnp.abs(jnp.linspace(min_decay, max_decay, 2 * HY_W, dtype=F32))[None, :]


def _bias_table(rpb):
    cols = np.arange(GRID_W)
    cstart = np.clip(cols - WIN_COLS // 2, 0, GRID_W - WIN_COLS)
    kc = np.arange(GRID_W)
    valid = (kc[None, :] >= cstart[:, None]) & (kc[None, :] < cstart[:, None] + WIN_COLS)
    rel = np.clip(kc[None, :] - cols[:, None] + (WIN_COLS - 1), 0, 2 * WIN_COLS - 2)
    t = jnp.where(valid[None, None], rpb.astype(F32)[:, :, rel] * LOG2E, NEG_INF)
    t = jnp.stack([t[:, :-1], t[:, 1:]], axis=3)
    t = t.reshape(N_HEADS, 2 * WIN_ROWS - 2, GRID_W, 2 * GRID_W).transpose(1, 0, 2, 3)
    return t.reshape(2 * WIN_ROWS - 2, N_HEADS // 2, 2 * GRID_W, 2 * GRID_W)


def _filter_spectrum(l, tabs, zpos2, absd, f_w1, f_b1, f_fr1, f_w2, f_b2, f_fr2, f_w3):
    n1 = tabs["n1"]
    w1p = jnp.pad(f_w1, ((0, 128 - f_w1.shape[0]), (0, 0)))
    kfilt = _filter(zpos2, w1p, f_b1[None], f_fr1[None], f_w2, f_b2[None], f_fr2[None], f_w3, absd, l)
    a = _slab_stage(tabs["w_filt"], kfilt.reshape(1, n1, DFT_N2, HY_W), BF16)
    return _mid_filter(a.reshape(2, n1, DFT_N2, HY_W), tabs["f2r"], tabs["f2i"], tabs["twr"], tabs["twi"])


def _layer(x, tabs, kf, bias_tab, bd, norm_g, w_in, conv_w, conv_b, hy_bias, qn_g, kn_g, on_hy, on_att, w_out):
    b, l, _ = x.shape
    t = b * l
    n1 = tabs["n1"]
    x2 = x.reshape(t, D_MODEL)
    qgain = jnp.tile(qn_g, N_HEADS)[None] * (HEAD_DIM ** -0.5 * LOG2E)
    kgain = jnp.tile(kn_g, N_HEADS)[None]
    z2, u, x0c = _inproj(x2, l, norm_g[None], w_in.astype(BF16), qgain, kgain, bd, conv_w, conv_b[None])
    z3 = z2.reshape(b, l, z2.shape[1])
    a = _slab_stage(tabs["w_data"], u.reshape(b // 2, n1, DFT_N2, HY_W), BF16)
    bp = _mid_data(a.reshape(b // 2, 2, n1, DFT_N2, HY_W), kf, tabs["f2r"], tabs["f2i"], tabs["twr"], tabs["twi"])
    ylong = _slab_stage(tabs["w_inv"], bp.reshape(b // 2, 2 * n1, DFT_N2, HY_W), F32)
    att = _attention(z3, bias_tab)
    out = _outproj(x2, ylong.reshape(t, HY_W), u.reshape(t, HY_W), x0c.reshape(t, HY_W), z2,
                   att.reshape(t, ATT_W), hy_bias[None], on_hy[None], on_att[None], w_out.astype(BF16))
    return out.reshape(b, l, D_MODEL)


def _trunk(x, norm_g, w_in, conv_w, conv_b, f_w1, f_b1, f_fr1, f_w2, f_b2, f_fr2, f_w3,
           hy_bias, qn_g, kn_g, rpb, on_hy, on_att, w_out):
    b, l, _ = x.shape
    assert b % 2 == 0 and l % (GRID_W * ATT_ROWS) == 0 and l // GRID_W >= WIN_ROWS
    assert (2 * l) % (DFT_N2 * MID_KB) == 0 and l % TOKEN_TILE == 0
    tabs = _dft_tables(l)
    zpos2 = _positional_features(l)
    absd = _abs_deltas()
    head = np.arange(CB) // HEAD_DIM
    bd = jnp.asarray((head[:, None] == head[None, :]).astype(np.float32) / HEAD_DIM, dtype=BF16)
    for i in range(norm_g.shape[0]):
        kf = _filter_spectrum(l, tabs, zpos2, absd, f_w1[i], f_b1[i], f_fr1[i], f_w2[i], f_b2[i], f_fr2[i], f_w3[i])
        x = _layer(x, tabs, kf, _bias_table(rpb[i]), bd, norm_g[i], w_in[i], conv_w[i], conv_b[i], hy_bias[i],
                   qn_g[i], kn_g[i], on_hy[i], on_att[i], w_out[i])
    return x


def kernel(x_prompt, x_sample, norm_g, w_in, conv_w, conv_b, f_w1, f_b1, f_fr1, f_w2, f_b2, f_fr2, f_w3,
           hy_bias, qn_g, kn_g, rpb, on_hy, on_att, w_out):
    params = (norm_g, w_in, conv_w, conv_b, f_w1, f_b1, f_fr1, f_w2, f_b2, f_fr2, f_w3,
              hy_bias, qn_g, kn_g, rpb, on_hy, on_att, w_out)
    return (_trunk(x_prompt, *params), _trunk(x_sample, *params))
```

```python
import functools
import math

import numpy as np
import jax
import jax.numpy as jnp
from jax import lax
from jax.experimental import pallas as pl
from jax.experimental.pallas import tpu as pltpu

F32 = jnp.float32
BF16 = jnp.bfloat16

D_MODEL = 1024
GRID_W = 64
HY_W = 512
ATT_W = 512
HEAD_DIM = 64
N_HEADS = 8
WIN_ROWS = 8
WIN_COLS = 16
POS_BANDS = 16
FILTER_HIDDEN = 64
N_IN = 4 * HY_W + 4 * ATT_W
EPS = 1e-6
NEG_INF = -1e30

CB = 512
DFT_N2 = 256
TOKEN_TILE = 512
N_HY_SLOTS = 3
Z_GATE_HY, Z_Q, Z_K, Z_V, Z_GATE_ATT = range(5)
ATT_ROWS = 8
KT_UNIT = 128
ATT_UNROLL = 2
LOG2E = math.log2(math.e)
MID_KB = 8
MID_COLS = 256
SUBLANES = 8
SLAB_ROWS = 16
SLAB_TARGET = 128


def _dot(a, b):
    return jnp.dot(a, b, preferred_element_type=F32)


def _dot_hi(a, b):
    return jnp.dot(a, b, preferred_element_type=F32, precision=lax.Precision.HIGHEST)


def _split(x):
    hi = x.astype(BF16)
    lo = (x - hi.astype(F32)).astype(BF16)
    return hi, lo


def _dot3(a, b):
    ah, al = _split(a)
    bh, bl = _split(b)
    return _dot(ah, bh) + (_dot(ah, bl) + _dot(al, bh))


def _inproj_kernel(x_ref, xp_ref, xn_ref, g_ref, w_ref, qg_ref, kg_ref, bd_ref, cw_ref, cb_ref,
                   z_ref, u_ref, x0_ref, *, tiles_per_seq):
    pos = pl.program_id(0) % tiles_per_seq
    tm = x_ref.shape[0]

    def norm(x):
        ms = jnp.mean(x * x, axis=-1, keepdims=True)
        return (x * lax.rsqrt(ms + EPS) * g_ref[...]).astype(BF16)

    h = norm(x_ref[...])
    hp = norm(xp_ref[...])
    hn = norm(xn_ref[...])
    rows = lax.broadcasted_iota(jnp.int32, (tm, CB), 0)
    conv = []
    for j in range(N_HY_SLOTS):
        cols = slice(j * CB, (j + 1) * CB)
        cur = _dot(h, w_ref[:, cols])
        prev_row = jnp.where(pos > 0, _dot(hp, w_ref[:, cols])[SUBLANES - 1:SUBLANES, :], 0.0)
        next_row = jnp.where(pos < tiles_per_seq - 1, _dot(hn, w_ref[:, cols])[0:1, :], 0.0)
        dn = jnp.where(rows == 0, prev_row, pltpu.roll(cur, 1, 0))
        up = jnp.where(rows == tm - 1, next_row, pltpu.roll(cur, tm - 1, 0))
        cw = cw_ref[:, cols]
        conv.append(cw[0:1, :] * dn + cw[1:2, :] * cur + cw[2:3, :] * up + cb_ref[:, cols])
    x0_ref[...] = conv[0].astype(x0_ref.dtype)
    u_ref[...] = (conv[1] * conv[2]).astype(u_ref.dtype)
    for j in range(N_HY_SLOTS, N_IN // CB):
        zj = _dot(h, w_ref[:, j * CB:(j + 1) * CB])
        if j in (4, 5):
            hm = _dot((zj * zj).astype(BF16), bd_ref[...])
            gain = qg_ref[...] if j == 4 else kg_ref[...]
            zj = zj * lax.rsqrt(hm + EPS) * gain
        z_ref[:, (j - N_HY_SLOTS) * CB:(j - N_HY_SLOTS + 1) * CB] = zj.astype(z_ref.dtype)


def _inproj(x2, seq_len, norm_g, w_in_bf, qgain, kgain, bd, conv_w, conv_b):
    t = x2.shape[0]
    tm = TOKEN_TILE
    hb = tm // SUBLANES
    nh = t // SUBLANES
    const = lambda i: (0, 0)
    row = lambda w: pl.BlockSpec((tm, w), lambda i: (i, 0))
    z_w = N_IN - N_HY_SLOTS * CB
    return pl.pallas_call(
        functools.partial(_inproj_kernel, tiles_per_seq=seq_len // tm),
        out_shape=(jax.ShapeDtypeStruct((t, z_w), BF16), jax.ShapeDtypeStruct((t, CB), BF16),
                   jax.ShapeDtypeStruct((t, CB), BF16)),
        grid=(t // tm,),
        in_specs=[
            row(D_MODEL),
            pl.BlockSpec((SUBLANES, D_MODEL), lambda i: (jnp.maximum(i * hb - 1, 0), 0)),
            pl.BlockSpec((SUBLANES, D_MODEL), lambda i: (jnp.minimum((i + 1) * hb, nh - 1), 0)),
            pl.BlockSpec((1, D_MODEL), const),
            pl.BlockSpec((D_MODEL, N_IN), const),
            pl.BlockSpec((1, CB), const),
            pl.BlockSpec((1, CB), const),
            pl.BlockSpec((CB, CB), const),
            pl.BlockSpec((3, N_HY_SLOTS * CB), const),
            pl.BlockSpec((1, N_HY_SLOTS * CB), const),
        ],
        out_specs=(row(z_w), row(CB), row(CB)),
        compiler_params=pltpu.CompilerParams(dimension_semantics=("arbitrary",)),
        name="inproj",
    )(x2, x2, x2, norm_g, w_in_bf, qgain, kgain, bd, conv_w, conv_b)


def _filter_kernel(zt_ref, t_ref, w1t_ref, b1_ref, fr1_ref, w2t_ref, b2_ref, fr2_ref, w3_ref, dl_ref, o_ref, *, l, tn):
    i = pl.program_id(0)
    h = jnp.sin(fr1_ref[...] * (_dot_hi(w1t_ref[...], zt_ref[...]) + b1_ref[...]))
    h = jnp.sin(fr2_ref[...] * (_dot_hi(w2t_ref[...], h) + b2_ref[...]))
    h = _dot3(h.T, w3_ref[...])
    h = h * jnp.exp(-t_ref[...] * dl_ref[...])
    row = i * tn + lax.broadcasted_iota(jnp.int32, h.shape, 0)
    o_ref[...] = jnp.where(row == l, 0.0, h).astype(o_ref.dtype)


def _filter(zpos2, w1p, b1, fr1, w2, b2, fr2, w3, absdeltas, l):
    n = 2 * l
    tn = min(512, l)
    nb_half = l // tn
    const = lambda i: (0, 0)
    half = lambda i: (0, i // nb_half)
    col = pl.BlockSpec((FILTER_HIDDEN, 1), const)
    return pl.pallas_call(
        functools.partial(_filter_kernel, l=l, tn=tn),
        out_shape=jax.ShapeDtypeStruct((n, HY_W), BF16),
        grid=(n // tn,),
        in_specs=[
            pl.BlockSpec((128, tn), lambda i: (0, i)),
            pl.BlockSpec((tn, 1), lambda i: (i, 0)),
            pl.BlockSpec((FILTER_HIDDEN, 128), const),
            col, col,
            pl.BlockSpec((FILTER_HIDDEN, FILTER_HIDDEN), const),
            col, col,
            pl.BlockSpec((FILTER_HIDDEN, HY_W), half),
            pl.BlockSpec((1, HY_W), half),
        ],
        out_specs=pl.BlockSpec((tn, HY_W), lambda i: (i, 0)),
        compiler_params=pltpu.CompilerParams(dimension_semantics=("arbitrary",)),
        name="hyena_filter",
    )(zpos2.T, zpos2[:, 0:1], w1p.T, b1.T, fr1.T, w2.T, b2.T, fr2.T, w3, absdeltas)


def _slab_kernel(w_ref, x_ref, o_ref, wk_scr):
    k, sub, c = x_ref.shape
    m = o_ref.shape[0]

    @pl.when((pl.program_id(0) == 0) & (pl.program_id(1) == 0))
    def _():
        m8, k8 = wk_scr.shape
        onehot = lambda cond: jnp.where(cond, 1.0, 0.0).astype(BF16)
        col_exp = onehot(lax.broadcasted_iota(jnp.int32, (k, k8), 1) // SUBLANES
                         == lax.broadcasted_iota(jnp.int32, (k, k8), 0))
        wcols = _dot(w_ref[...].astype(BF16), col_exp).astype(BF16)
        rb = min(m8, 256)
        for r in range(m8 // rb):
            rows = r * rb + lax.broadcasted_iota(jnp.int32, (rb, m), 0)
            row_exp = onehot(rows // SUBLANES == lax.broadcasted_iota(jnp.int32, (rb, m), 1))
            blk = _dot(row_exp, wcols)
            same = (lax.broadcasted_iota(jnp.int32, (rb, k8), 0) % SUBLANES
                    == lax.broadcasted_iota(jnp.int32, (rb, k8), 1) % SUBLANES)
            wk_scr[r * rb:(r + 1) * rb, :] = jnp.where(same, blk, 0.0).astype(BF16)

    x = x_ref[...].astype(F32)
    halves = []
    for h in range(sub // SUBLANES):
        xh = x[:, h * SUBLANES:(h + 1) * SUBLANES, :].reshape(k * SUBLANES, c)
        halves.append(_dot(wk_scr[...], xh.astype(BF16)).reshape(m, SUBLANES, c))
    o_ref[...] = jnp.concatenate(halves, axis=1).astype(o_ref.dtype)


def _slab_stage(w, x, out_dtype):
    p, k, n2, c = x.shape
    m = w.shape[0]
    assert w.shape == (m, k)
    rows = SLAB_ROWS * max(1, SLAB_TARGET // max(m, k))
    return pl.pallas_call(
        _slab_kernel,
        out_shape=jax.ShapeDtypeStruct((p, m, n2, c), out_dtype),
        grid=(p, n2 // rows),
        in_specs=[pl.BlockSpec((m, k), lambda pi, j: (0, 0)),
                  pl.BlockSpec((None, k, rows, c), lambda pi, j: (pi, 0, j, 0))],
        out_specs=pl.BlockSpec((None, m, rows, c), lambda pi, j: (pi, 0, j, 0)),
        scratch_shapes=[pltpu.VMEM((m * SUBLANES, k * SUBLANES), BF16)],
        compiler_params=pltpu.CompilerParams(dimension_semantics=("arbitrary", "arbitrary")),
        name="dft_outer",
    )(w, x)


def _stacked_g(f2r, f2i, tr, ti):
    gr = f2r * tr - f2i * ti
    gi = f2r * ti + f2i * tr
    return gr, gi


def _stack(a, b, c, d):
    return jnp.concatenate([jnp.concatenate([a, b], axis=1), jnp.concatenate([c, d], axis=1)], axis=0)


def _mid_data_kernel(a_ref, kf_ref, f2r_ref, f2i_ref, twr_ref, twi_ref, o_ref, gs_scr, gh_scr):
    @pl.when(pl.program_id(1) == 0)
    def _():
        f2r = f2r_ref[...]
        f2i = f2i_ref[...]
        for j in range(MID_KB):
            gr, gi = _stacked_g(f2r, f2i, twr_ref[j:j + 1, :], twi_ref[j:j + 1, :])
            gs_scr[j] = _stack(gr, -gi, gi, gr).astype(BF16)
            grt, git = gr.T, gi.T
            gh_scr[j] = _stack(grt, git, -git, grt).astype(BF16)

    c = a_ref.shape[-1]
    for j in range(MID_KB):
        for cb in range(c // MID_COLS):
            cols = slice(cb * MID_COLS, (cb + 1) * MID_COLS)
            a = jnp.concatenate([a_ref[0, j, :, cols], a_ref[1, j, :, cols]], axis=0)
            x = _dot(gs_scr[j], a)
            xr, xi = x[:DFT_N2], x[DFT_N2:]
            kr, ki = kf_ref[0, j, :, cols].astype(F32), kf_ref[1, j, :, cols].astype(F32)
            y = jnp.concatenate([xr * kr - xi * ki, xr * ki + xi * kr], axis=0).astype(BF16)
            bp = _dot(gh_scr[j], y)
            o_ref[0, j, :, cols] = bp[:DFT_N2].astype(o_ref.dtype)
            o_ref[1, j, :, cols] = bp[DFT_N2:].astype(o_ref.dtype)


def _mid_filter_kernel(a_ref, f2r_ref, f2i_ref, twr_ref, twi_ref, o_ref):
    f2r = f2r_ref[...]
    f2i = f2i_ref[...]
    for j in range(MID_KB):
        gr, gi = _stacked_g(f2r, f2i, twr_ref[j:j + 1, :], twi_ref[j:j + 1, :])
        gs = _stack(gr, -gi, gi, gr).astype(BF16)
        a = jnp.concatenate([a_ref[0, j], a_ref[1, j]], axis=0)
        x = _dot(gs, a)
        o_ref[0, j] = x[:DFT_N2].astype(o_ref.dtype)
        o_ref[1, j] = x[DFT_N2:].astype(o_ref.dtype)


def _mid_specs(n1):
    sq = pl.BlockSpec((DFT_N2, DFT_N2), lambda k, p: (0, 0))
    tw = pl.BlockSpec((MID_KB, DFT_N2), lambda k, p: (k, 0))
    return sq, tw


def _mid_data(a5, kf, f2r, f2i, twr, twi):
    p, _, n1, n2, c = a5.shape
    sq, tw = _mid_specs(n1)
    blk = pl.BlockSpec((None, 2, MID_KB, n2, c), lambda k, pi: (pi, 0, k, 0, 0))
    return pl.pallas_call(
        _mid_data_kernel,
        out_shape=jax.ShapeDtypeStruct(a5.shape, BF16),
        grid=(n1 // MID_KB, p),
        in_specs=[blk, pl.BlockSpec((2, MID_KB, n2, c), lambda k, pi: (0, k, 0, 0)), sq, sq, tw, tw],
        out_specs=blk,
        scratch_shapes=[pltpu.VMEM((MID_KB, 2 * n2, 2 * n2), BF16), pltpu.VMEM((MID_KB, 2 * n2, 2 * n2), BF16)],
        compiler_params=pltpu.CompilerParams(dimension_semantics=("arbitrary", "arbitrary")),
        name="dft_mid",
    )(a5, kf, f2r, f2i, twr, twi)


def _mid_filter(a4, f2r, f2i, twr, twi):
    _, n1, n2, c = a4.shape
    sq, tw = _mid_specs(n1)
    blk = pl.BlockSpec((2, MID_KB, n2, c), lambda k, pi: (0, k, 0, 0))
    return pl.pallas_call(
        _mid_filter_kernel,
        out_shape=jax.ShapeDtypeStruct(a4.shape, BF16),
        grid=(n1 // MID_KB, 1),
        in_specs=[blk, sq, sq, tw, tw],
        out_specs=blk,
        compiler_params=pltpu.CompilerParams(dimension_semantics=("arbitrary", "arbitrary")),
        name="dft_mid_filter",
    )(a4, f2r, f2i, twr, twi)


def _attn_kernel(q_ref, k_ref, v_ref, bias_ref, o_ref, kt_scr, s_scr, m_scr, *, rows):
    rb = pl.program_id(1)
    lane = lax.broadcasted_iota(jnp.int32, (GRID_W, 128), 1)
    first = lane < HEAD_DIM
    band = WIN_ROWS * GRID_W
    ones = jnp.ones((band, 128), BF16)
    n_keys = rows * GRID_W
    units_per_block = ATT_ROWS * GRID_W // KT_UNIT
    n_trips = ATT_ROWS // ATT_UNROLL
    assert units_per_block == n_trips
    pairs = range(N_HEADS // 2)

    def transpose_unit(m):
        lanes = pl.ds(pl.multiple_of(m * KT_UNIT, KT_UNIT), KT_UNIT)
        kt_scr[0, :, lanes] = k_ref[pl.ds(pl.multiple_of(m * KT_UNIT, KT_UNIT), KT_UNIT), :].T
        src = jnp.minimum(m * KT_UNIT + GRID_W, n_keys - KT_UNIT)
        kt_scr[1, :, lanes] = k_ref[pl.ds(pl.multiple_of(src, GRID_W), KT_UNIT), :].T

    @pl.when(rb == 0)
    def _():
        for j in range(units_per_block):
            transpose_unit(j)

    nxt = jnp.minimum(rb + 1, pl.num_programs(1) - 1)

    def row_info(t):
        info = []
        for a in range(ATT_UNROLL):
            rl = t * ATT_UNROLL + a
            r = rb * ATT_ROWS + rl
            r0 = jnp.clip(r - WIN_ROWS // 2, 0, rows - WIN_ROWS)
            par = r0 % 2
            info.append((r - r0, par, (r0 - par) * GRID_W, r0 * GRID_W, pl.multiple_of(rl * GRID_W, GRID_W)))
        return info

    def scores(t, slot):
        for a, (d, par, kt_start, start, qoff) in enumerate(row_info(t)):
            for pr in pairs:
                cs = slice(pr * 128, (pr + 1) * 128)
                q2 = q_ref[pl.ds(qoff, GRID_W), cs]
                zero = jnp.zeros_like(q2)
                qbd = jnp.concatenate([jnp.where(first, q2, zero), jnp.where(first, zero, q2)], axis=0)
                bias = [bias_ref[2 * j + (WIN_ROWS - 1) - d, pr] for j in range(band // KT_UNIT)]
                kl = pl.ds(pl.multiple_of(kt_start, KT_UNIT), band)
                s = _dot(qbd, kt_scr[par, cs, kl]) + jnp.concatenate(bias, axis=1)
                s_scr[slot, a, pr] = s
                m_scr[slot, a, pr] = jnp.broadcast_to(jnp.max(s, axis=-1, keepdims=True), (2 * GRID_W, 128))

    def outputs(t, slot):
        for a, (d, par, kt_start, start, qoff) in enumerate(row_info(t)):
            for pr in pairs:
                cs = slice(pr * 128, (pr + 1) * 128)
                m = m_scr[slot, a, pr]
                p = jnp.exp2(s_scr[slot, a, pr] - jnp.concatenate([m] * (band // 128), axis=1)).astype(BF16)
                ks = pl.ds(pl.multiple_of(start, GRID_W), band)
                acc = _dot(p, jnp.concatenate([v_ref[ks, cs], ones], axis=1))
                o = acc[:, :128] / acc[:, 128:]
                o_ref[pl.ds(qoff, GRID_W), cs] = jnp.where(first, o[:GRID_W], o[GRID_W:]).astype(o_ref.dtype)

    scores(0, 0)
    for t in range(n_trips):
        transpose_unit(nxt * units_per_block + t)
        if t + 1 < n_trips:
            scores(t + 1, (t + 1) % 2)
        outputs(t, t % 2)


def _attention(z3, bias_tab):
    b, l, _ = z3.shape
    rows = l // GRID_W
    tq = ATT_ROWS * GRID_W
    return pl.pallas_call(
        functools.partial(_attn_kernel, rows=rows),
        out_shape=jax.ShapeDtypeStruct((b, l, ATT_W), BF16),
        grid=(b, rows // ATT_ROWS),
        in_specs=[
            pl.BlockSpec((None, tq, CB), lambda bi, i: (bi, i, Z_Q)),
            pl.BlockSpec((None, l, CB), lambda bi, i: (bi, 0, Z_K), pipeline_mode=pl.Buffered(1)),
            pl.BlockSpec((None, l, CB), lambda bi, i: (bi, 0, Z_V)),
            pl.BlockSpec(bias_tab.shape, lambda bi, i: (0, 0, 0, 0), pipeline_mode=pl.Buffered(1)),
        ],
        out_specs=pl.BlockSpec((None, tq, ATT_W), lambda bi, i: (bi, i, 0)),
        scratch_shapes=[pltpu.VMEM((2, ATT_W, l), BF16),
                        pltpu.VMEM((2, ATT_UNROLL, N_HEADS // 2, 2 * GRID_W, WIN_ROWS * GRID_W), F32),
                        pltpu.VMEM((2, ATT_UNROLL, N_HEADS // 2, 2 * GRID_W, 128), F32)],
        compiler_params=pltpu.CompilerParams(dimension_semantics=("arbitrary", "arbitrary")),
        name="nbr_attention",
    )(z3, z3, z3, bias_tab)


def _outproj_kernel(x_ref, yl_ref, u_ref, x0_ref, gh_ref, at_ref, ga_ref, hb_ref, onh_ref, ona_ref, w_ref, o_ref):
    yh = x0_ref[...].astype(F32) * (yl_ref[...].astype(F32) + u_ref[...].astype(F32) * hb_ref[...])
    yh = yh * lax.rsqrt(jnp.mean(yh * yh, axis=-1, keepdims=True) + EPS) * onh_ref[...]
    yh = yh * jax.nn.silu(gh_ref[...].astype(F32))
    ya = at_ref[...].astype(F32)
    ya = ya * lax.rsqrt(jnp.mean(ya * ya, axis=-1, keepdims=True) + EPS) * ona_ref[...]
    ya = ya * jax.nn.silu(ga_ref[...].astype(F32))
    acc = _dot(yh.astype(BF16), w_ref[:HY_W, :]) + _dot(ya.astype(BF16), w_ref[HY_W:, :])
    o_ref[...] = x_ref[...] + acc


def _outproj(x2, ylong, u, x0c, z2, att, hy_bias, on_hy, on_att, w_out_bf):
    t = x2.shape[0]
    tm = TOKEN_TILE
    row = lambda c: pl.BlockSpec((tm, CB), lambda i, c=c: (i, c))
    vec = pl.BlockSpec((1, CB), lambda i: (0, 0))
    return pl.pallas_call(
        _outproj_kernel,
        out_shape=jax.ShapeDtypeStruct((t, D_MODEL), F32),
        grid=(t // tm,),
        in_specs=[pl.BlockSpec((tm, D_MODEL), lambda i: (i, 0)), row(0), row(0), row(0), row(Z_GATE_HY), row(0),
                  row(Z_GATE_ATT),
                  vec, vec, vec, pl.BlockSpec((HY_W + ATT_W, D_MODEL), lambda i: (0, 0))],
        out_specs=pl.BlockSpec((tm, D_MODEL), lambda i: (i, 0)),
        compiler_params=pltpu.CompilerParams(dimension_semantics=("arbitrary",)),
        name="outproj",
    )(x2, ylong, u, x0c, z2, att, z2, hy_bias, on_hy, on_att, w_out_bf)


def _dft_tables(l):
    n = 2 * l
    n2 = DFT_N2
    n1 = n // n2
    h = n1 // 2
    k1 = np.arange(n1)
    ang1 = -2.0 * np.pi * ((k1[:, None] * k1[None, :]) % n1) / n1
    f1r, f1i = np.cos(ang1), np.sin(ang1)
    w_data = np.block([[f1r[:, :h], -f1i[:, :h]], [f1i[:, :h], f1r[:, :h]]])
    w_filt = np.concatenate([f1r, f1i], axis=0)
    w_inv = np.block([[f1r[:h], f1i[:h]], [-f1i[:h], f1r[:h]]]) / n
    k2 = np.arange(n2)
    ang2 = -2.0 * np.pi * ((k2[:, None] * k2[None, :]) % n2) / n2
    angt = -2.0 * np.pi * (k1[:, None] * k2[None, :]) / n
    f = lambda a: jnp.asarray(a, dtype=F32)
    return dict(n1=n1, w_data=f(w_data), w_filt=f(w_filt), w_inv=f(w_inv),
                f2r=f(np.cos(ang2)), f2i=f(np.sin(ang2)), twr=f(np.cos(angt)), twi=f(np.sin(angt)))


def _positional_features(l):
    t01 = jnp.linspace(0.0, 1.0, l, dtype=F32)[:, None]
    w = 2.0 * math.pi * jnp.arange(l, dtype=F32)[:, None] / l
    f = jnp.linspace(1e-4, POS_BANDS - 1, POS_BANDS, dtype=F32)[None, :]
    z = jnp.concatenate([t01, jnp.cos(f * w), -jnp.sin(f * w)], axis=-1)
    z2 = jnp.concatenate([z, jnp.zeros((1, z.shape[1]), F32), z[::-1][:l - 1]], axis=0)
    return jnp.pad(z2, ((0, 0), (0, 128 - z2.shape[1])))


def _abs_deltas():
    min_decay = math.log(1e-2) / 1.5
    max_decay = math.log(1e-2) / 0.3
    return jnp.abs(jnp.linspace(min_decay, max_decay, 2 * HY_W, dtype=F32))[None, :]


def _bias_table(rpb):
    cols = np.arange(GRID_W)
    cstart = np.clip(cols - WIN_COLS // 2, 0, GRID_W - WIN_COLS)
    kc = np.arange(GRID_W)
    valid = (kc[None, :] >= cstart[:, None]) & (kc[None, :] < cstart[:, None] + WIN_COLS)
    rel = np.clip(kc[None, :] - cols[:, None] + (WIN_COLS - 1), 0, 2 * WIN_COLS - 2)
    t = jnp.where(valid[None, None], rpb.astype(F32)[:, :, rel] * LOG2E, NEG_INF)
    t = jnp.stack([t[:, :-1], t[:, 1:]], axis=3)
    t = t.reshape(N_HEADS, 2 * WIN_ROWS - 2, GRID_W, 2 * GRID_W).transpose(1, 0, 2, 3)
    return t.reshape(2 * WIN_ROWS - 2, N_HEADS // 2, 2 * GRID_W, 2 * GRID_W)


def _filter_spectrum(l, tabs, zpos2, absd, f_w1, f_b1, f_fr1, f_w2, f_b2, f_fr2, f_w3):
    n1 = tabs["n1"]
    w1p = jnp.pad(f_w1, ((0, 128 - f_w1.shape[0]), (0, 0)))
    kfilt = _filter(zpos2, w1p, f_b1[None], f_fr1[None], f_w2, f_b2[None], f_fr2[None], f_w3, absd, l)
    a = _slab_stage(tabs["w_filt"], kfilt.reshape(1, n1, DFT_N2, HY_W), BF16)
    return _mid_filter(a.reshape(2, n1, DFT_N2, HY_W), tabs["f2r"], tabs["f2i"], tabs["twr"], tabs["twi"])


def _layer(x, tabs, kf, bias_tab, bd, norm_g, w_in, conv_w, conv_b, hy_bias, qn_g, kn_g, on_hy, on_att, w_out):
    b, l, _ = x.shape
    t = b * l
    n1 = tabs["n1"]
    x2 = x.reshape(t, D_MODEL)
    qgain = jnp.tile(qn_g, N_HEADS)[None] * (HEAD_DIM ** -0.5 * LOG2E)
    kgain = jnp.tile(kn_g, N_HEADS)[None]
    z2, u, x0c = _inproj(x2, l, norm_g[None], w_in.astype(BF16), qgain, kgain, bd, conv_w, conv_b[None])
    z3 = z2.reshape(b, l, z2.shape[1])
    a = _slab_stage(tabs["w_data"], u.reshape(b // 2, n1, DFT_N2, HY_W), BF16)
    bp = _mid_data(a.reshape(b // 2, 2, n1, DFT_N2, HY_W), kf, tabs["f2r"], tabs["f2i"], tabs["twr"], tabs["twi"])
    ylong = _slab_stage(tabs["w_inv"], bp.reshape(b // 2, 2 * n1, DFT_N2, HY_W), BF16)
    att = _attention(z3, bias_tab)
    out = _outproj(x2, ylong.reshape(t, HY_W), u.reshape(t, HY_W), x0c.reshape(t, HY_W), z2,
                   att.reshape(t, ATT_W), hy_bias[None], on_hy[None], on_att[None], w_out.astype(BF16))
    return out.reshape(b, l, D_MODEL)


def _trunk(x, norm_g, w_in, conv_w, conv_b, f_w1, f_b1, f_fr1, f_w2, f_b2, f_fr2, f_w3,
           hy_bias, qn_g, kn_g, rpb, on_hy, on_att, w_out):
    b, l, _ = x.shape
    assert b % 2 == 0 and l % (GRID_W * ATT_ROWS) == 0 and l // GRID_W >= WIN_ROWS
    assert (2 * l) % (DFT_N2 * MID_KB) == 0 and l % TOKEN_TILE == 0
    tabs = _dft_tables(l)
    zpos2 = _positional_features(l)
    absd = _abs_deltas()
    head = np.arange(CB) // HEAD_DIM
    bd = jnp.asarray((head[:, None] == head[None, :]).astype(np.float32) / HEAD_DIM, dtype=BF16)
    for i in range(norm_g.shape[0]):
        kf = _filter_spectrum(l, tabs, zpos2, absd, f_w1[i], f_b1[i], f_fr1[i], f_w2[i], f_b2[i], f_fr2[i], f_w3[i])
        x = _layer(x, tabs, kf, _bias_table(rpb[i]), bd, norm_g[i], w_in[i], conv_w[i], conv_b[i], hy_bias[i],
                   qn_g[i], kn_g[i], on_hy[i], on_att[i], w_out[i])
    return x


def kernel(x_prompt, x_sample, norm_g, w_in, conv_w, conv_b, f_w1, f_b1, f_fr1, f_w2, f_b2, f_fr2, f_w3,
           hy_bias, qn_g, kn_g, rpb, on_hy, on_att, w_out):
    params = (norm_g, w_in, conv_w, conv_b, f_w1, f_b1, f_fr1, f_w2, f_b2, f_fr2, f_w3,
              hy_bias, qn_g, kn_g, rpb, on_hy, on_att, w_out)
    return (_trunk(x_prompt, *params), _trunk(x_sample, *params))
```

```python
import functools
import math

import numpy as np
import jax
import jax.numpy as jnp
from jax import lax
from jax.experimental import pallas as pl
from jax.experimental.pallas import tpu as pltpu

F32 = jnp.float32
BF16 = jnp.bfloat16

D_MODEL = 1024
GRID_W = 64
HY_W = 512
ATT_W = 512
HEAD_DIM = 64
N_HEADS = 8
WIN_ROWS = 8
WIN_COLS = 16
POS_BANDS = 16
FILTER_HIDDEN = 64
N_IN = 4 * HY_W + 4 * ATT_W
EPS = 1e-6
NEG_INF = -1e30

CB = 512
DFT_N2 = 256
TOKEN_TILE = 1024
N_HY_SLOTS = 3
Z_GATE_HY, Z_Q, Z_K, Z_V, Z_GATE_ATT = range(5)
ATT_ROWS = 8
KT_UNIT = 128
ATT_UNROLL = 2
LOG2E = math.log2(math.e)
MID_KB = 8
MID_COLS = 256
SUBLANES = 8
SLAB_ROWS = 16
SLAB_TARGET = 128


def _dot(a, b):
    return jnp.dot(a, b, preferred_element_type=F32)


def _dot_hi(a, b):
    return jnp.dot(a, b, preferred_element_type=F32, precision=lax.Precision.HIGHEST)


def _split(x):
    hi = x.astype(BF16)
    lo = (x - hi.astype(F32)).astype(BF16)
    return hi, lo


def _dot3(a, b):
    ah, al = _split(a)
    bh, bl = _split(b)
    return _dot(ah, bh) + (_dot(ah, bl) + _dot(al, bh))


def _inproj_kernel(x_ref, xp_ref, xn_ref, g_ref, w_ref, qg_ref, kg_ref, bd_ref, cw_ref, cb_ref,
                   z_ref, u_ref, x0_ref, *, tiles_per_seq):
    pos = pl.program_id(0) % tiles_per_seq
    tm = x_ref.shape[0]

    def norm(x):
        ms = jnp.mean(x * x, axis=-1, keepdims=True)
        return (x * lax.rsqrt(ms + EPS) * g_ref[...]).astype(BF16)

    h = norm(x_ref[...])
    hp = norm(xp_ref[...])
    hn = norm(xn_ref[...])
    rows = lax.broadcasted_iota(jnp.int32, (tm, CB), 0)
    conv = []
    for j in range(N_HY_SLOTS):
        cols = slice(j * CB, (j + 1) * CB)
        cur = _dot(h, w_ref[:, cols])
        prev_row = jnp.where(pos > 0, _dot(hp, w_ref[:, cols])[SUBLANES - 1:SUBLANES, :], 0.0)
        next_row = jnp.where(pos < tiles_per_seq - 1, _dot(hn, w_ref[:, cols])[0:1, :], 0.0)
        dn = jnp.where(rows == 0, prev_row, pltpu.roll(cur, 1, 0))
        up = jnp.where(rows == tm - 1, next_row, pltpu.roll(cur, tm - 1, 0))
        cw = cw_ref[:, cols]
        conv.append(cw[0:1, :] * dn + cw[1:2, :] * cur + cw[2:3, :] * up + cb_ref[:, cols])
    x0_ref[...] = conv[0].astype(x0_ref.dtype)
    u_ref[...] = (conv[1] * conv[2]).astype(u_ref.dtype)
    for j in range(N_HY_SLOTS, N_IN // CB):
        zj = _dot(h, w_ref[:, j * CB:(j + 1) * CB])
        if j in (4, 5):
            hm = _dot((zj * zj).astype(BF16), bd_ref[...])
            gain = qg_ref[...] if j == 4 else kg_ref[...]
            zj = zj * lax.rsqrt(hm + EPS) * gain
        z_ref[:, (j - N_HY_SLOTS) * CB:(j - N_HY_SLOTS + 1) * CB] = zj.astype(z_ref.dtype)


def _inproj(x2, seq_len, norm_g, w_in_bf, qgain, kgain, bd, conv_w, conv_b):
    t = x2.shape[0]
    tm = TOKEN_TILE
    hb = tm // SUBLANES
    nh = t // SUBLANES
    const = lambda i: (0, 0)
    row = lambda w: pl.BlockSpec((tm, w), lambda i: (i, 0))
    z_w = N_IN - N_HY_SLOTS * CB
    return pl.pallas_call(
        functools.partial(_inproj_kernel, tiles_per_seq=seq_len // tm),
        out_shape=(jax.ShapeDtypeStruct((t, z_w), BF16), jax.ShapeDtypeStruct((t, CB), BF16),
                   jax.ShapeDtypeStruct((t, CB), BF16)),
        grid=(t // tm,),
        in_specs=[
            row(D_MODEL),
            pl.BlockSpec((SUBLANES, D_MODEL), lambda i: (jnp.maximum(i * hb - 1, 0), 0)),
            pl.BlockSpec((SUBLANES, D_MODEL), lambda i: (jnp.minimum((i + 1) * hb, nh - 1), 0)),
            pl.BlockSpec((1, D_MODEL), const),
            pl.BlockSpec((D_MODEL, N_IN), const),
            pl.BlockSpec((1, CB), const),
            pl.BlockSpec((1, CB), const),
            pl.BlockSpec((CB, CB), const),
            pl.BlockSpec((3, N_HY_SLOTS * CB), const),
            pl.BlockSpec((1, N_HY_SLOTS * CB), const),
        ],
        out_specs=(row(z_w), row(CB), row(CB)),
        compiler_params=pltpu.CompilerParams(dimension_semantics=("arbitrary",)),
        name="inproj",
    )(x2, x2, x2, norm_g, w_in_bf, qgain, kgain, bd, conv_w, conv_b)


def _filter_kernel(zt_ref, t_ref, w1t_ref, b1_ref, fr1_ref, w2t_ref, b2_ref, fr2_ref, w3_ref, dl_ref,
                   fwd_ref, bwd_ref, carry_ref):
    tn = fwd_ref.shape[0]

    @pl.when(pl.program_id(0) == 0)
    def _():
        carry_ref[...] = jnp.zeros_like(carry_ref)

    h = jnp.sin(fr1_ref[...] * (_dot_hi(w1t_ref[...], zt_ref[...]) + b1_ref[...]))
    h = jnp.sin(fr2_ref[...] * (_dot_hi(w2t_ref[...], h) + b2_ref[...]))
    h = _dot3(h.T, w3_ref[...])
    h = h * jnp.exp(-t_ref[...] * dl_ref[...])
    fwd_ref[...] = h[:, :HY_W].astype(fwd_ref.dtype)
    hb = h[:, HY_W:].astype(BF16)
    i_out = lax.broadcasted_iota(jnp.int32, (tn, tn), 0)
    i_in = lax.broadcasted_iota(jnp.int32, (tn, tn), 1)
    flip = jnp.where((i_out + i_in == tn) & (i_out > 0), 1.0, 0.0).astype(BF16)
    rev = _dot(flip, hb)
    first_row = lax.broadcasted_iota(jnp.int32, rev.shape, 0) == 0
    bwd_ref[...] = jnp.where(first_row, carry_ref[0:1, :], rev).astype(bwd_ref.dtype)
    carry_ref[...] = hb[:carry_ref.shape[0], :].astype(F32)


def _filter(zpos, w1p, b1, fr1, w2, b2, fr2, w3, absdeltas):
    l = zpos.shape[0]
    tn = min(512, l)
    nb = l // tn
    const = lambda i: (0, 0)
    col = pl.BlockSpec((FILTER_HIDDEN, 1), const)
    half = jax.ShapeDtypeStruct((l, HY_W), BF16)
    return pl.pallas_call(
        _filter_kernel,
        out_shape=(half, half),
        grid=(nb,),
        in_specs=[
            pl.BlockSpec((128, tn), lambda i: (0, nb - 1 - i)),
            pl.BlockSpec((tn, 1), lambda i: (nb - 1 - i, 0)),
            pl.BlockSpec((FILTER_HIDDEN, 128), const),
            col, col,
            pl.BlockSpec((FILTER_HIDDEN, FILTER_HIDDEN), const),
            col, col,
            pl.BlockSpec((FILTER_HIDDEN, 2 * HY_W), const),
            pl.BlockSpec((1, 2 * HY_W), const),
        ],
        out_specs=(pl.BlockSpec((tn, HY_W), lambda i: (nb - 1 - i, 0)), pl.BlockSpec((tn, HY_W), lambda i: (i, 0))),
        scratch_shapes=[pltpu.VMEM((SUBLANES, HY_W), F32)],
        compiler_params=pltpu.CompilerParams(dimension_semantics=("arbitrary",)),
        name="hyena_filter",
    )(zpos.T, zpos[:, 0:1], w1p.T, b1.T, fr1.T, w2.T, b2.T, fr2.T, w3, absdeltas)


def _slab_kernel(w_ref, *refs):
    x_refs, o_ref, wk_scr = refs[:-2], refs[-2], refs[-1]
    k = sum(r.shape[0] for r in x_refs)
    _, sub, c = x_refs[0].shape
    m = o_ref.shape[0]

    @pl.when((pl.program_id(0) == 0) & (pl.program_id(1) == 0))
    def _():
        m8, k8 = wk_scr.shape
        onehot = lambda cond: jnp.where(cond, 1.0, 0.0).astype(BF16)
        col_exp = onehot(lax.broadcasted_iota(jnp.int32, (k, k8), 1) // SUBLANES
                         == lax.broadcasted_iota(jnp.int32, (k, k8), 0))
        wcols = _dot(w_ref[...].astype(BF16), col_exp).astype(BF16)
        rb = min(m8, 256)
        for r in range(m8 // rb):
            rows = r * rb + lax.broadcasted_iota(jnp.int32, (rb, m), 0)
            row_exp = onehot(rows // SUBLANES == lax.broadcasted_iota(jnp.int32, (rb, m), 1))
            blk = _dot(row_exp, wcols)
            same = (lax.broadcasted_iota(jnp.int32, (rb, k8), 0) % SUBLANES
                    == lax.broadcasted_iota(jnp.int32, (rb, k8), 1) % SUBLANES)
            wk_scr[r * rb:(r + 1) * rb, :] = jnp.where(same, blk, 0.0).astype(BF16)

    x = jnp.concatenate([r[...].astype(F32) for r in x_refs], axis=0)
    halves = []
    for h in range(sub // SUBLANES):
        xh = x[:, h * SUBLANES:(h + 1) * SUBLANES, :].reshape(k * SUBLANES, c)
        halves.append(_dot(wk_scr[...], xh.astype(BF16)).reshape(m, SUBLANES, c))
    o_ref[...] = jnp.concatenate(halves, axis=1).astype(o_ref.dtype)


def _slab_stage(w, xs, out_dtype):
    p, _, n2, c = xs[0].shape
    k = sum(x.shape[1] for x in xs)
    m = w.shape[0]
    assert w.shape == (m, k)
    rows = SLAB_ROWS * max(1, SLAB_TARGET // max(m, k))
    return pl.pallas_call(
        _slab_kernel,
        out_shape=jax.ShapeDtypeStruct((p, m, n2, c), out_dtype),
        grid=(p, n2 // rows),
        in_specs=[pl.BlockSpec((m, k), lambda pi, j: (0, 0))]
        + [pl.BlockSpec((None, x.shape[1], rows, c), lambda pi, j: (pi, 0, j, 0)) for x in xs],
        out_specs=pl.BlockSpec((None, m, rows, c), lambda pi, j: (pi, 0, j, 0)),
        scratch_shapes=[pltpu.VMEM((m * SUBLANES, k * SUBLANES), BF16)],
        compiler_params=pltpu.CompilerParams(dimension_semantics=("arbitrary", "arbitrary")),
        name="dft_outer",
    )(w, *xs)


def _stacked_g(f2r, f2i, tr, ti):
    gr = f2r * tr - f2i * ti
    gi = f2r * ti + f2i * tr
    return gr, gi


def _stack(a, b, c, d):
    return jnp.concatenate([jnp.concatenate([a, b], axis=1), jnp.concatenate([c, d], axis=1)], axis=0)


def _mid_data_kernel(a_ref, kf_ref, f2r_ref, f2i_ref, twr_ref, twi_ref, o_ref, gs_scr, gh_scr):
    @pl.when(pl.program_id(1) == 0)
    def _():
        f2r = f2r_ref[...]
        f2i = f2i_ref[...]
        for j in range(MID_KB):
            gr, gi = _stacked_g(f2r, f2i, twr_ref[j:j + 1, :], twi_ref[j:j + 1, :])
            gs_scr[j] = _stack(gr, -gi, gi, gr).astype(BF16)
            grt, git = gr.T, gi.T
            gh_scr[j] = _stack(grt, git, -git, grt).astype(BF16)

    c = a_ref.shape[-1]
    for j in range(MID_KB):
        for cb in range(c // MID_COLS):
            cols = slice(cb * MID_COLS, (cb + 1) * MID_COLS)
            a = jnp.concatenate([a_ref[0, j, :, cols], a_ref[1, j, :, cols]], axis=0)
            x = _dot(gs_scr[j], a)
            xr, xi = x[:DFT_N2], x[DFT_N2:]
            kr, ki = kf_ref[0, j, :, cols].astype(F32), kf_ref[1, j, :, cols].astype(F32)
            y = jnp.concatenate([xr * kr - xi * ki, xr * ki + xi * kr], axis=0).astype(BF16)
            bp = _dot(gh_scr[j], y)
            o_ref[0, j, :, cols] = bp[:DFT_N2].astype(o_ref.dtype)
            o_ref[1, j, :, cols] = bp[DFT_N2:].astype(o_ref.dtype)


def _mid_filter_kernel(a_ref, f2r_ref, f2i_ref, twr_ref, twi_ref, o_ref):
    f2r = f2r_ref[...]
    f2i = f2i_ref[...]
    for j in range(MID_KB):
        gr, gi = _stacked_g(f2r, f2i, twr_ref[j:j + 1, :], twi_ref[j:j + 1, :])
        gs = _stack(gr, -gi, gi, gr).astype(BF16)
        a = jnp.concatenate([a_ref[0, j], a_ref[1, j]], axis=0)
        x = _dot(gs, a)
        o_ref[0, j] = x[:DFT_N2].astype(o_ref.dtype)
        o_ref[1, j] = x[DFT_N2:].astype(o_ref.dtype)


def _mid_specs(n1):
    sq = pl.BlockSpec((DFT_N2, DFT_N2), lambda k, p: (0, 0))
    tw = pl.BlockSpec((MID_KB, DFT_N2), lambda k, p: (k, 0))
    return sq, tw


def _mid_data(a5, kf, f2r, f2i, twr, twi):
    p, _, n1, n2, c = a5.shape
    sq, tw = _mid_specs(n1)
    blk = pl.BlockSpec((None, 2, MID_KB, n2, c), lambda k, pi: (pi, 0, k, 0, 0))
    return pl.pallas_call(
        _mid_data_kernel,
        out_shape=jax.ShapeDtypeStruct(a5.shape, BF16),
        grid=(n1 // MID_KB, p),
        in_specs=[blk, pl.BlockSpec((2, MID_KB, n2, c), lambda k, pi: (0, k, 0, 0)), sq, sq, tw, tw],
        out_specs=blk,
        scratch_shapes=[pltpu.VMEM((MID_KB, 2 * n2, 2 * n2), BF16), pltpu.VMEM((MID_KB, 2 * n2, 2 * n2), BF16)],
        compiler_params=pltpu.CompilerParams(dimension_semantics=("arbitrary", "arbitrary")),
        name="dft_mid",
    )(a5, kf, f2r, f2i, twr, twi)


def _mid_filter(a4, f2r, f2i, twr, twi):
    _, n1, n2, c = a4.shape
    sq, tw = _mid_specs(n1)
    blk = pl.BlockSpec((2, MID_KB, n2, c), lambda k, pi: (0, k, 0, 0))
    return pl.pallas_call(
        _mid_filter_kernel,
        out_shape=jax.ShapeDtypeStruct(a4.shape, BF16),
        grid=(n1 // MID_KB, 1),
        in_specs=[blk, sq, sq, tw, tw],
        out_specs=blk,
        compiler_params=pltpu.CompilerParams(dimension_semantics=("arbitrary", "arbitrary")),
        name="dft_mid_filter",
    )(a4, f2r, f2i, twr, twi)


def _attn_kernel(q_ref, k_ref, v_ref, bias_ref, o_ref, kt_scr, s_scr, m_scr, *, rows):
    rb = pl.program_id(1)
    lane = lax.broadcasted_iota(jnp.int32, (GRID_W, 128), 1)
    first = lane < HEAD_DIM
    band = WIN_ROWS * GRID_W
    ones = jnp.ones((band, 128), BF16)
    n_keys = rows * GRID_W
    units_per_block = ATT_ROWS * GRID_W // KT_UNIT
    n_trips = ATT_ROWS // ATT_UNROLL
    assert units_per_block == n_trips
    pairs = range(N_HEADS // 2)

    def transpose_unit(m):
        lanes = pl.ds(pl.multiple_of(m * KT_UNIT, KT_UNIT), KT_UNIT)
        kt_scr[0, :, lanes] = k_ref[pl.ds(pl.multiple_of(m * KT_UNIT, KT_UNIT), KT_UNIT), :].T
        src = jnp.minimum(m * KT_UNIT + GRID_W, n_keys - KT_UNIT)
        kt_scr[1, :, lanes] = k_ref[pl.ds(pl.multiple_of(src, GRID_W), KT_UNIT), :].T

    @pl.when(rb == 0)
    def _():
        for j in range(units_per_block):
            transpose_unit(j)

    nxt = jnp.minimum(rb + 1, pl.num_programs(1) - 1)

    def row_info(t):
        info = []
        for a in range(ATT_UNROLL):
            rl = t * ATT_UNROLL + a
            r = rb * ATT_ROWS + rl
            r0 = jnp.clip(r - WIN_ROWS // 2, 0, rows - WIN_ROWS)
            par = r0 % 2
            info.append((r - r0, par, (r0 - par) * GRID_W, r0 * GRID_W, pl.multiple_of(rl * GRID_W, GRID_W)))
        return info

    def scores(t, slot):
        for a, (d, par, kt_start, start, qoff) in enumerate(row_info(t)):
            for pr in pairs:
                cs = slice(pr * 128, (pr + 1) * 128)
                q2 = q_ref[pl.ds(qoff, GRID_W), cs]
                zero = jnp.zeros_like(q2)
                qbd = jnp.concatenate([jnp.where(first, q2, zero), jnp.where(first, zero, q2)], axis=0)
                bias = [bias_ref[2 * j + (WIN_ROWS - 1) - d, pr] for j in range(band // KT_UNIT)]
                kl = pl.ds(pl.multiple_of(kt_start, KT_UNIT), band)
                s = _dot(qbd, kt_scr[par, cs, kl]) + jnp.concatenate(bias, axis=1)
                s_scr[slot, a, pr] = s
                m_scr[slot, a, pr] = jnp.broadcast_to(jnp.max(s, axis=-1, keepdims=True), (2 * GRID_W, 128))

    def outputs(t, slot):
        for a, (d, par, kt_start, start, qoff) in enumerate(row_info(t)):
            for pr in pairs:
                cs = slice(pr * 128, (pr + 1) * 128)
                m = m_scr[slot, a, pr]
                p = jnp.exp2(s_scr[slot, a, pr] - jnp.concatenate([m] * (band // 128), axis=1)).astype(BF16)
                ks = pl.ds(pl.multiple_of(start, GRID_W), band)
                acc = _dot(p, jnp.concatenate([v_ref[ks, cs], ones], axis=1))
                o = acc[:, :128] / acc[:, 128:]
                o_ref[pl.ds(qoff, GRID_W), cs] = jnp.where(first, o[:GRID_W], o[GRID_W:]).astype(o_ref.dtype)

    scores(0, 0)
    for t in range(n_trips):
        transpose_unit(nxt * units_per_block + t)
        if t + 1 < n_trips:
            scores(t + 1, (t + 1) % 2)
        outputs(t, t % 2)


def _attention(z3, bias_tab):
    b, l, _ = z3.shape
    rows = l // GRID_W
    tq = ATT_ROWS * GRID_W
    return pl.pallas_call(
        functools.partial(_attn_kernel, rows=rows),
        out_shape=jax.ShapeDtypeStruct((b, l, ATT_W), BF16),
        grid=(b, rows // ATT_ROWS),
        in_specs=[
            pl.BlockSpec((None, tq, CB), lambda bi, i: (bi, i, Z_Q)),
            pl.BlockSpec((None, l, CB), lambda bi, i: (bi, 0, Z_K), pipeline_mode=pl.Buffered(1)),
            pl.BlockSpec((None, l, CB), lambda bi, i: (bi, 0, Z_V)),
            pl.BlockSpec(bias_tab.shape, lambda bi, i: (0, 0, 0, 0), pipeline_mode=pl.Buffered(1)),
        ],
        out_specs=pl.BlockSpec((None, tq, ATT_W), lambda bi, i: (bi, i, 0)),
        scratch_shapes=[pltpu.VMEM((2, ATT_W, l), BF16),
                        pltpu.VMEM((2, ATT_UNROLL, N_HEADS // 2, 2 * GRID_W, WIN_ROWS * GRID_W), F32),
                        pltpu.VMEM((2, ATT_UNROLL, N_HEADS // 2, 2 * GRID_W, 128), F32)],
        compiler_params=pltpu.CompilerParams(dimension_semantics=("arbitrary", "arbitrary")),
        name="nbr_attention",
    )(z3, z3, z3, bias_tab)


def _outproj_kernel(x_ref, yl_ref, u_ref, x0_ref, gh_ref, at_ref, ga_ref, hb_ref, onh_ref, ona_ref, w_ref, o_ref):
    yh = x0_ref[...].astype(F32) * (yl_ref[...].astype(F32) + u_ref[...].astype(F32) * hb_ref[...])
    yh = yh * lax.rsqrt(jnp.mean(yh * yh, axis=-1, keepdims=True) + EPS) * onh_ref[...]
    yh = yh * jax.nn.silu(gh_ref[...].astype(F32))
    ya = at_ref[...].astype(F32)
    ya = ya * lax.rsqrt(jnp.mean(ya * ya, axis=-1, keepdims=True) + EPS) * ona_ref[...]
    ya = ya * jax.nn.silu(ga_ref[...].astype(F32))
    acc = _dot(yh.astype(BF16), w_ref[:HY_W, :]) + _dot(ya.astype(BF16), w_ref[HY_W:, :])
    o_ref[...] = x_ref[...] + acc


def _outproj(x2, ylong, u, x0c, z2, att, hy_bias, on_hy, on_att, w_out_bf):
    t = x2.shape[0]
    tm = TOKEN_TILE
    row = lambda c: pl.BlockSpec((tm, CB), lambda i, c=c: (i, c))
    vec = pl.BlockSpec((1, CB), lambda i: (0, 0))
    return pl.pallas_call(
        _outproj_kernel,
        out_shape=jax.ShapeDtypeStruct((t, D_MODEL), F32),
        grid=(t // tm,),
        in_specs=[pl.BlockSpec((tm, D_MODEL), lambda i: (i, 0)), row(0), row(0), row(0), row(Z_GATE_HY), row(0),
                  row(Z_GATE_ATT),
                  vec, vec, vec, pl.BlockSpec((HY_W + ATT_W, D_MODEL), lambda i: (0, 0))],
        out_specs=pl.BlockSpec((tm, D_MODEL), lambda i: (i, 0)),
        compiler_params=pltpu.CompilerParams(dimension_semantics=("arbitrary",)),
        name="outproj",
    )(x2, ylong, u, x0c, z2, att, z2, hy_bias, on_hy, on_att, w_out_bf)


def _dft_tables(l):
    n = 2 * l
    n2 = DFT_N2
    n1 = n // n2
    h = n1 // 2
    k1 = np.arange(n1)
    ang1 = -2.0 * np.pi * ((k1[:, None] * k1[None, :]) % n1) / n1
    f1r, f1i = np.cos(ang1), np.sin(ang1)
    w_data = np.block([[f1r[:, :h], -f1i[:, :h]], [f1i[:, :h], f1r[:, :h]]])
    w_filt = np.concatenate([f1r, f1i], axis=0)
    w_inv = np.block([[f1r[:h], f1i[:h]], [-f1i[:h], f1r[:h]]]) / n
    k2 = np.arange(n2)
    ang2 = -2.0 * np.pi * ((k2[:, None] * k2[None, :]) % n2) / n2
    angt = -2.0 * np.pi * (k1[:, None] * k2[None, :]) / n
    f = lambda a: jnp.asarray(a, dtype=F32)
    return dict(n1=n1, w_data=f(w_data), w_filt=f(w_filt), w_inv=f(w_inv),
                f2r=f(np.cos(ang2)), f2i=f(np.sin(ang2)), twr=f(np.cos(angt)), twi=f(np.sin(angt)))


def _positional_features(l):
    t01 = jnp.linspace(0.0, 1.0, l, dtype=F32)[:, None]
    w = 2.0 * math.pi * jnp.arange(l, dtype=F32)[:, None] / l
    f = jnp.linspace(1e-4, POS_BANDS - 1, POS_BANDS, dtype=F32)[None, :]
    z = jnp.concatenate([t01, jnp.cos(f * w), -jnp.sin(f * w)], axis=-1)
    return jnp.pad(z, ((0, 0), (0, 128 - z.shape[1])))


def _abs_deltas():
    min_decay = math.log(1e-2) / 1.5
    max_decay = math.log(1e-2) / 0.3
    return jnp.abs(jnp.linspace(min_decay, max_decay, 2 * HY_W, dtype=F32))[None, :]


def _bias_table(rpb):
    cols = np.arange(GRID_W)
    cstart = np.clip(cols - WIN_COLS // 2, 0, GRID_W - WIN_COLS)
    kc = np.arange(GRID_W)
    valid = (kc[None, :] >= cstart[:, None]) & (kc[None, :] < cstart[:, None] + WIN_COLS)
    rel = np.clip(kc[None, :] - cols[:, None] + (WIN_COLS - 1), 0, 2 * WIN_COLS - 2)
    t = jnp.where(valid[None, None], rpb.astype(F32)[:, :, rel] * LOG2E, NEG_INF)
    t = jnp.stack([t[:, :-1], t[:, 1:]], axis=3)
    t = t.reshape(N_HEADS, 2 * WIN_ROWS - 2, GRID_W, 2 * GRID_W).transpose(1, 0, 2, 3)
    return t.reshape(2 * WIN_ROWS - 2, N_HEADS // 2, 2 * GRID_W, 2 * GRID_W)


def _filter_spectrum(tabs, zpos, absd, f_w1, f_b1, f_fr1, f_w2, f_b2, f_fr2, f_w3):
    n1 = tabs["n1"]
    w1p = jnp.pad(f_w1, ((0, 128 - f_w1.shape[0]), (0, 0)))
    halves = _filter(zpos, w1p, f_b1[None], f_fr1[None], f_w2, f_b2[None], f_fr2[None], f_w3, absd)
    a = _slab_stage(tabs["w_filt"], [h.reshape(1, n1 // 2, DFT_N2, HY_W) for h in halves], BF16)
    return _mid_filter(a.reshape(2, n1, DFT_N2, HY_W), tabs["f2r"], tabs["f2i"], tabs["twr"], tabs["twi"])


def _layer(x, tabs, kf, bias_tab, bd, norm_g, w_in, conv_w, conv_b, hy_bias, qn_g, kn_g, on_hy, on_att, w_out):
    b, l, _ = x.shape
    t = b * l
    n1 = tabs["n1"]
    x2 = x.reshape(t, D_MODEL)
    qgain = jnp.tile(qn_g, N_HEADS)[None] * (HEAD_DIM ** -0.5 * LOG2E)
    kgain = jnp.tile(kn_g, N_HEADS)[None]
    z2, u, x0c = _inproj(x2, l, norm_g[None], w_in.astype(BF16), qgain, kgain, bd, conv_w, conv_b[None])
    z3 = z2.reshape(b, l, z2.shape[1])
    a = _slab_stage(tabs["w_data"], [u.reshape(b // 2, n1, DFT_N2, HY_W)], BF16)
    bp = _mid_data(a.reshape(b // 2, 2, n1, DFT_N2, HY_W), kf, tabs["f2r"], tabs["f2i"], tabs["twr"], tabs["twi"])
    ylong = _slab_stage(tabs["w_inv"], [bp.reshape(b // 2, 2 * n1, DFT_N2, HY_W)], BF16)
    att = _attention(z3, bias_tab)
    out = _outproj(x2, ylong.reshape(t, HY_W), u.reshape(t, HY_W), x0c.reshape(t, HY_W), z2,
                   att.reshape(t, ATT_W), hy_bias[None], on_hy[None], on_att[None], w_out.astype(BF16))
    return out.reshape(b, l, D_MODEL)


def _trunk(x, norm_g, w_in, conv_w, conv_b, f_w1, f_b1, f_fr1, f_w2, f_b2, f_fr2, f_w3,
           hy_bias, qn_g, kn_g, rpb, on_hy, on_att, w_out):
    b, l, _ = x.shape
    assert b % 2 == 0 and l % (GRID_W * ATT_ROWS) == 0 and l // GRID_W >= WIN_ROWS
    assert (2 * l) % (DFT_N2 * MID_KB) == 0 and l % TOKEN_TILE == 0
    tabs = _dft_tables(l)
    zpos = _positional_features(l)
    absd = _abs_deltas()
    head = np.arange(CB) // HEAD_DIM
    bd = jnp.asarray((head[:, None] == head[None, :]).astype(np.float32) / HEAD_DIM, dtype=BF16)
    for i in range(norm_g.shape[0]):
        kf = _filter_spectrum(tabs, zpos, absd, f_w1[i], f_b1[i], f_fr1[i], f_w2[i], f_b2[i], f_fr2[i], f_w3[i])
        x = _layer(x, tabs, kf, _bias_table(rpb[i]), bd, norm_g[i], w_in[i], conv_w[i], conv_b[i], hy_bias[i],
                   qn_g[i], kn_g[i], on_hy[i], on_att[i], w_out[i])
    return x


def kernel(x_prompt, x_sample, norm_g, w_in, conv_w, conv_b, f_w1, f_b1, f_fr1, f_w2, f_b2, f_fr2, f_w3,
           hy_bias, qn_g, kn_g, rpb, on_hy, on_att, w_out):
    params = (norm_g, w_in, conv_w, conv_b, f_w1, f_b1, f_fr1, f_w2, f_b2, f_fr2, f_w3,
              hy_bias, qn_g, kn_g, rpb, on_hy, on_att, w_out)
    return (_trunk(x_prompt, *params), _trunk(x_sample, *params))
```

```python
import functools
import math

import numpy as np
import jax
import jax.numpy as jnp
from jax import lax
from jax.experimental import pallas as pl
from jax.experimental.pallas import tpu as pltpu

F32 = jnp.float32
BF16 = jnp.bfloat16

D_MODEL = 1024
GRID_W = 64
HY_W = 512
ATT_W = 512
HEAD_DIM = 64
N_HEADS = 8
WIN_ROWS = 8
WIN_COLS = 16
POS_BANDS = 16
FILTER_HIDDEN = 64
N_IN = 4 * HY_W + 4 * ATT_W
EPS = 1e-6
NEG_INF = -1e30

CB = 512
DFT_N2 = 256
TOKEN_TILE = 1024
N_HY_SLOTS = 3
Z_GATE_HY, Z_Q, Z_K, Z_V, Z_GATE_ATT = range(5)
ATT_ROWS = 8
KT_UNIT = 128
ATT_UNROLL = 2
LOG2E = math.log2(math.e)
MID_KB = 8
MID_COLS = 256
SUBLANES = 8
SLAB_ROWS = 16
SLAB_TARGET = 128


def _dot(a, b):
    return jnp.dot(a, b, preferred_element_type=F32)


def _dot_hi(a, b):
    return jnp.dot(a, b, preferred_element_type=F32, precision=lax.Precision.HIGHEST)


def _split(x):
    hi = x.astype(BF16)
    lo = (x - hi.astype(F32)).astype(BF16)
    return hi, lo


def _dot3(a, b):
    ah, al = _split(a)
    bh, bl = _split(b)
    return _dot(ah, bh) + (_dot(ah, bl) + _dot(al, bh))


def _inproj_kernel(x_ref, xp_ref, xn_ref, g_ref, w_ref, qg_ref, kg_ref, bd_ref, cw_ref, cb_ref,
                   z_ref, u_ref, x0_ref, *, tiles_per_seq):
    pos = pl.program_id(0) % tiles_per_seq
    tm = x_ref.shape[0]

    def norm(x):
        ms = jnp.mean(x * x, axis=-1, keepdims=True)
        return (x * lax.rsqrt(ms + EPS) * g_ref[...]).astype(BF16)

    h = norm(x_ref[...])
    hp = norm(xp_ref[...])
    hn = norm(xn_ref[...])
    rows = lax.broadcasted_iota(jnp.int32, (tm, CB), 0)
    conv = []
    for j in range(N_HY_SLOTS):
        cols = slice(j * CB, (j + 1) * CB)
        cur = _dot(h, w_ref[:, cols])
        prev_row = jnp.where(pos > 0, _dot(hp, w_ref[:, cols])[SUBLANES - 1:SUBLANES, :], 0.0)
        next_row = jnp.where(pos < tiles_per_seq - 1, _dot(hn, w_ref[:, cols])[0:1, :], 0.0)
        dn = jnp.where(rows == 0, prev_row, pltpu.roll(cur, 1, 0))
        up = jnp.where(rows == tm - 1, next_row, pltpu.roll(cur, tm - 1, 0))
        cw = cw_ref[:, cols]
        conv.append(cw[0:1, :] * dn + cw[1:2, :] * cur + cw[2:3, :] * up + cb_ref[:, cols])
    x0_ref[...] = conv[0].astype(x0_ref.dtype)
    u_ref[...] = (conv[1] * conv[2]).astype(u_ref.dtype)
    for j in range(N_HY_SLOTS, N_IN // CB):
        zj = _dot(h, w_ref[:, j * CB:(j + 1) * CB])
        if j in (4, 5):
            hm = _dot((zj * zj).astype(BF16), bd_ref[...])
            gain = qg_ref[...] if j == 4 else kg_ref[...]
            zj = zj * lax.rsqrt(hm + EPS) * gain
        z_ref[:, (j - N_HY_SLOTS) * CB:(j - N_HY_SLOTS + 1) * CB] = zj.astype(z_ref.dtype)


def _inproj(x2, seq_len, norm_g, w_in_bf, qgain, kgain, bd, conv_w, conv_b):
    t = x2.shape[0]
    tm = TOKEN_TILE
    hb = tm // SUBLANES
    nh = t // SUBLANES
    const = lambda i: (0, 0)
    row = lambda w: pl.BlockSpec((tm, w), lambda i: (i, 0))
    z_w = N_IN - N_HY_SLOTS * CB
    return pl.pallas_call(
        functools.partial(_inproj_kernel, tiles_per_seq=seq_len // tm),
        out_shape=(jax.ShapeDtypeStruct((t, z_w), BF16), jax.ShapeDtypeStruct((t, CB), BF16),
                   jax.ShapeDtypeStruct((t, CB), BF16)),
        grid=(t // tm,),
        in_specs=[
            row(D_MODEL),
            pl.BlockSpec((SUBLANES, D_MODEL), lambda i: (jnp.maximum(i * hb - 1, 0), 0)),
            pl.BlockSpec((SUBLANES, D_MODEL), lambda i: (jnp.minimum((i + 1) * hb, nh - 1), 0)),
            pl.BlockSpec((1, D_MODEL), const),
            pl.BlockSpec((D_MODEL, N_IN), const),
            pl.BlockSpec((1, CB), const),
            pl.BlockSpec((1, CB), const),
            pl.BlockSpec((CB, CB), const),
            pl.BlockSpec((3, N_HY_SLOTS * CB), const),
            pl.BlockSpec((1, N_HY_SLOTS * CB), const),
        ],
        out_specs=(row(z_w), row(CB), row(CB)),
        compiler_params=pltpu.CompilerParams(dimension_semantics=("arbitrary",)),
        name="inproj",
    )(x2, x2, x2, norm_g, w_in_bf, qgain, kgain, bd, conv_w, conv_b)


def _filter_kernel(zt_ref, t_ref, w1t_ref, b1_ref, fr1_ref, w2t_ref, b2_ref, fr2_ref, w3_ref, dl_ref,
                   fwd_ref, bwd_ref, carry_ref):
    tn = fwd_ref.shape[0]

    @pl.when(pl.program_id(0) == 0)
    def _():
        carry_ref[...] = jnp.zeros_like(carry_ref)

    h = jnp.sin(fr1_ref[...] * (_dot_hi(w1t_ref[...], zt_ref[...]) + b1_ref[...]))
    h = jnp.sin(fr2_ref[...] * (_dot_hi(w2t_ref[...], h) + b2_ref[...]))
    h = _dot3(h.T, w3_ref[...])
    h = h * jnp.exp(-t_ref[...] * dl_ref[...])
    fwd_ref[...] = h[:, :HY_W].astype(fwd_ref.dtype)
    hb = h[:, HY_W:].astype(BF16)
    i_out = lax.broadcasted_iota(jnp.int32, (tn, tn), 0)
    i_in = lax.broadcasted_iota(jnp.int32, (tn, tn), 1)
    flip = jnp.where((i_out + i_in == tn) & (i_out > 0), 1.0, 0.0).astype(BF16)
    rev = _dot(flip, hb)
    first_row = lax.broadcasted_iota(jnp.int32, rev.shape, 0) == 0
    bwd_ref[...] = jnp.where(first_row, carry_ref[0:1, :], rev).astype(bwd_ref.dtype)
    carry_ref[...] = hb[:carry_ref.shape[0], :].astype(F32)


def _filter(zpos, w1p, b1, fr1, w2, b2, fr2, w3, absdeltas):
    zpos_t, tcol = zpos
    l = tcol.shape[0]
    tn = min(512, l)
    nb = l // tn
    const = lambda i: (0, 0)
    col = pl.BlockSpec((FILTER_HIDDEN, 1), const)
    half = jax.ShapeDtypeStruct((l, HY_W), BF16)
    return pl.pallas_call(
        _filter_kernel,
        out_shape=(half, half),
        grid=(nb,),
        in_specs=[
            pl.BlockSpec((128, tn), lambda i: (0, nb - 1 - i)),
            pl.BlockSpec((tn, 1), lambda i: (nb - 1 - i, 0)),
            pl.BlockSpec((FILTER_HIDDEN, 128), const),
            col, col,
            pl.BlockSpec((FILTER_HIDDEN, FILTER_HIDDEN), const),
            col, col,
            pl.BlockSpec((FILTER_HIDDEN, 2 * HY_W), const),
            pl.BlockSpec((1, 2 * HY_W), const),
        ],
        out_specs=(pl.BlockSpec((tn, HY_W), lambda i: (nb - 1 - i, 0)), pl.BlockSpec((tn, HY_W), lambda i: (i, 0))),
        scratch_shapes=[pltpu.VMEM((SUBLANES, HY_W), F32)],
        compiler_params=pltpu.CompilerParams(dimension_semantics=("arbitrary",)),
        name="hyena_filter",
    )(zpos_t, tcol, w1p.T, b1.T, fr1.T, w2.T, b2.T, fr2.T, w3, absdeltas)


def _slab_kernel(w_ref, *refs):
    x_refs, o_ref, wk_scr = refs[:-2], refs[-2], refs[-1]
    k = sum(r.shape[0] for r in x_refs)
    _, sub, c = x_refs[0].shape
    m = o_ref.shape[0]

    @pl.when((pl.program_id(0) == 0) & (pl.program_id(1) == 0))
    def _():
        m8, k8 = wk_scr.shape
        onehot = lambda cond: jnp.where(cond, 1.0, 0.0).astype(BF16)
        col_exp = onehot(lax.broadcasted_iota(jnp.int32, (k, k8), 1) // SUBLANES
                         == lax.broadcasted_iota(jnp.int32, (k, k8), 0))
        wcols = _dot(w_ref[...].astype(BF16), col_exp).astype(BF16)
        rb = min(m8, 256)
        for r in range(m8 // rb):
            rows = r * rb + lax.broadcasted_iota(jnp.int32, (rb, m), 0)
            row_exp = onehot(rows // SUBLANES == lax.broadcasted_iota(jnp.int32, (rb, m), 1))
            blk = _dot(row_exp, wcols)
            same = (lax.broadcasted_iota(jnp.int32, (rb, k8), 0) % SUBLANES
                    == lax.broadcasted_iota(jnp.int32, (rb, k8), 1) % SUBLANES)
            wk_scr[r * rb:(r + 1) * rb, :] = jnp.where(same, blk, 0.0).astype(BF16)

    x = jnp.concatenate([r[...].astype(F32) for r in x_refs], axis=0)
    halves = []
    for h in range(sub // SUBLANES):
        xh = x[:, h * SUBLANES:(h + 1) * SUBLANES, :].reshape(k * SUBLANES, c)
        halves.append(_dot(wk_scr[...], xh.astype(BF16)).reshape(m, SUBLANES, c))
    o_ref[...] = jnp.concatenate(halves, axis=1).astype(o_ref.dtype)


def _slab_stage(w, xs, out_dtype):
    p, _, n2, c = xs[0].shape
    k = sum(x.shape[1] for x in xs)
    m = w.shape[0]
    assert w.shape == (m, k)
    rows = SLAB_ROWS * max(1, SLAB_TARGET // max(m, k))
    return pl.pallas_call(
        _slab_kernel,
        out_shape=jax.ShapeDtypeStruct((p, m, n2, c), out_dtype),
        grid=(p, n2 // rows),
        in_specs=[pl.BlockSpec((m, k), lambda pi, j: (0, 0))]
        + [pl.BlockSpec((None, x.shape[1], rows, c), lambda pi, j: (pi, 0, j, 0)) for x in xs],
        out_specs=pl.BlockSpec((None, m, rows, c), lambda pi, j: (pi, 0, j, 0)),
        scratch_shapes=[pltpu.VMEM((m * SUBLANES, k * SUBLANES), BF16)],
        compiler_params=pltpu.CompilerParams(dimension_semantics=("arbitrary", "arbitrary")),
        name="dft_outer",
    )(w, *xs)


def _stacked_g(f2r, f2i, tr, ti):
    gr = f2r * tr - f2i * ti
    gi = f2r * ti + f2i * tr
    return gr, gi


def _stack(a, b, c, d):
    return jnp.concatenate([jnp.concatenate([a, b], axis=1), jnp.concatenate([c, d], axis=1)], axis=0)


def _mid_data_kernel(a_ref, kf_ref, f2r_ref, f2i_ref, twr_ref, twi_ref, o_ref, gs_scr, gh_scr):
    @pl.when(pl.program_id(1) == 0)
    def _():
        f2r = f2r_ref[...]
        f2i = f2i_ref[...]
        for j in range(MID_KB):
            gr, gi = _stacked_g(f2r, f2i, twr_ref[j:j + 1, :], twi_ref[j:j + 1, :])
            gs_scr[j] = _stack(gr, -gi, gi, gr).astype(BF16)
            grt, git = gr.T, gi.T
            gh_scr[j] = _stack(grt, git, -git, grt).astype(BF16)

    c = a_ref.shape[-1]
    for j in range(MID_KB):
        for cb in range(c // MID_COLS):
            cols = slice(cb * MID_COLS, (cb + 1) * MID_COLS)
            a = jnp.concatenate([a_ref[0, j, :, cols], a_ref[1, j, :, cols]], axis=0)
            x = _dot(gs_scr[j], a)
            xr, xi = x[:DFT_N2], x[DFT_N2:]
            kr, ki = kf_ref[0, j, :, cols].astype(F32), kf_ref[1, j, :, cols].astype(F32)
            y = jnp.concatenate([xr * kr - xi * ki, xr * ki + xi * kr], axis=0).astype(BF16)
            bp = _dot(gh_scr[j], y)
            o_ref[0, j, :, cols] = bp[:DFT_N2].astype(o_ref.dtype)
            o_ref[1, j, :, cols] = bp[DFT_N2:].astype(o_ref.dtype)


def _mid_filter_kernel(a_ref, f2r_ref, f2i_ref, twr_ref, twi_ref, o_ref):
    f2r = f2r_ref[...]
    f2i = f2i_ref[...]
    for j in range(MID_KB):
        gr, gi = _stacked_g(f2r, f2i, twr_ref[j:j + 1, :], twi_ref[j:j + 1, :])
        gs = _stack(gr, -gi, gi, gr).astype(BF16)
        a = jnp.concatenate([a_ref[0, j], a_ref[1, j]], axis=0)
        x = _dot(gs, a)
        o_ref[0, j] = x[:DFT_N2].astype(o_ref.dtype)
        o_ref[1, j] = x[DFT_N2:].astype(o_ref.dtype)


def _mid_specs(n1):
    sq = pl.BlockSpec((DFT_N2, DFT_N2), lambda k, p: (0, 0))
    tw = pl.BlockSpec((MID_KB, DFT_N2), lambda k, p: (k, 0))
    return sq, tw


def _mid_data(a5, kf, f2r, f2i, twr, twi):
    p, _, n1, n2, c = a5.shape
    sq, tw = _mid_specs(n1)
    blk = pl.BlockSpec((None, 2, MID_KB, n2, c), lambda k, pi: (pi, 0, k, 0, 0))
    return pl.pallas_call(
        _mid_data_kernel,
        out_shape=jax.ShapeDtypeStruct(a5.shape, BF16),
        grid=(n1 // MID_KB, p),
        in_specs=[blk, pl.BlockSpec((2, MID_KB, n2, c), lambda k, pi: (0, k, 0, 0)), sq, sq, tw, tw],
        out_specs=blk,
        scratch_shapes=[pltpu.VMEM((MID_KB, 2 * n2, 2 * n2), BF16), pltpu.VMEM((MID_KB, 2 * n2, 2 * n2), BF16)],
        compiler_params=pltpu.CompilerParams(dimension_semantics=("arbitrary", "arbitrary")),
        name="dft_mid",
    )(a5, kf, f2r, f2i, twr, twi)


def _mid_filter(a4, f2r, f2i, twr, twi):
    _, n1, n2, c = a4.shape
    sq, tw = _mid_specs(n1)
    blk = pl.BlockSpec((2, MID_KB, n2, c), lambda k, pi: (0, k, 0, 0))
    return pl.pallas_call(
        _mid_filter_kernel,
        out_shape=jax.ShapeDtypeStruct(a4.shape, BF16),
        grid=(n1 // MID_KB, 1),
        in_specs=[blk, sq, sq, tw, tw],
        out_specs=blk,
        compiler_params=pltpu.CompilerParams(dimension_semantics=("arbitrary", "arbitrary")),
        name="dft_mid_filter",
    )(a4, f2r, f2i, twr, twi)


def _attn_kernel(q_ref, k_ref, v_ref, bias_ref, o_ref, kt_scr, s_scr, m_scr, *, rows):
    rb = pl.program_id(1)
    lane = lax.broadcasted_iota(jnp.int32, (GRID_W, 128), 1)
    first = lane < HEAD_DIM
    band = WIN_ROWS * GRID_W
    ones = jnp.ones((band, 128), BF16)
    n_keys = rows * GRID_W
    units_per_block = ATT_ROWS * GRID_W // KT_UNIT
    n_trips = ATT_ROWS // ATT_UNROLL
    assert units_per_block == n_trips
    pairs = range(N_HEADS // 2)

    def transpose_unit(m):
        lanes = pl.ds(pl.multiple_of(m * KT_UNIT, KT_UNIT), KT_UNIT)
        kt_scr[0, :, lanes] = k_ref[pl.ds(pl.multiple_of(m * KT_UNIT, KT_UNIT), KT_UNIT), :].T
        src = jnp.minimum(m * KT_UNIT + GRID_W, n_keys - KT_UNIT)
        kt_scr[1, :, lanes] = k_ref[pl.ds(pl.multiple_of(src, GRID_W), KT_UNIT), :].T

    @pl.when(rb == 0)
    def _():
        for j in range(units_per_block):
            transpose_unit(j)

    nxt = jnp.minimum(rb + 1, pl.num_programs(1) - 1)

    def row_info(t):
        info = []
        for a in range(ATT_UNROLL):
            rl = t * ATT_UNROLL + a
            r = rb * ATT_ROWS + rl
            r0 = jnp.clip(r - WIN_ROWS // 2, 0, rows - WIN_ROWS)
            par = r0 % 2
            info.append((r - r0, par, (r0 - par) * GRID_W, r0 * GRID_W, pl.multiple_of(rl * GRID_W, GRID_W)))
        return info

    def scores(t, slot):
        for a, (d, par, kt_start, start, qoff) in enumerate(row_info(t)):
            for pr in pairs:
                cs = slice(pr * 128, (pr + 1) * 128)
                q2 = q_ref[pl.ds(qoff, GRID_W), cs]
                zero = jnp.zeros_like(q2)
                qbd = jnp.concatenate([jnp.where(first, q2, zero), jnp.where(first, zero, q2)], axis=0)
                bias = [bias_ref[2 * j + (WIN_ROWS - 1) - d, pr] for j in range(band // KT_UNIT)]
                kl = pl.ds(pl.multiple_of(kt_start, KT_UNIT), band)
                s = _dot(qbd, kt_scr[par, cs, kl]) + jnp.concatenate(bias, axis=1)
                s_scr[slot, a, pr] = s
                m_scr[slot, a, pr] = jnp.broadcast_to(jnp.max(s, axis=-1, keepdims=True), (2 * GRID_W, 128))

    def outputs(t, slot):
        for a, (d, par, kt_start, start, qoff) in enumerate(row_info(t)):
            for pr in pairs:
                cs = slice(pr * 128, (pr + 1) * 128)
                m = m_scr[slot, a, pr]
                p = jnp.exp2(s_scr[slot, a, pr] - jnp.concatenate([m] * (band // 128), axis=1)).astype(BF16)
                ks = pl.ds(pl.multiple_of(start, GRID_W), band)
                acc = _dot(p, jnp.concatenate([v_ref[ks, cs], ones], axis=1))
                o = acc[:, :128] / acc[:, 128:]
                o_ref[pl.ds(qoff, GRID_W), cs] = jnp.where(first, o[:GRID_W], o[GRID_W:]).astype(o_ref.dtype)

    scores(0, 0)
    for t in range(n_trips):
        transpose_unit(nxt * units_per_block + t)
        if t + 1 < n_trips:
            scores(t + 1, (t + 1) % 2)
        outputs(t, t % 2)


def _attention(z3, bias_tab):
    b, l, _ = z3.shape
    rows = l // GRID_W
    tq = ATT_ROWS * GRID_W
    return pl.pallas_call(
        functools.partial(_attn_kernel, rows=rows),
        out_shape=jax.ShapeDtypeStruct((b, l, ATT_W), BF16),
        grid=(b, rows // ATT_ROWS),
        in_specs=[
            pl.BlockSpec((None, tq, CB), lambda bi, i: (bi, i, Z_Q)),
            pl.BlockSpec((None, l, CB), lambda bi, i: (bi, 0, Z_K), pipeline_mode=pl.Buffered(1)),
            pl.BlockSpec((None, l, CB), lambda bi, i: (bi, 0, Z_V)),
            pl.BlockSpec(bias_tab.shape, lambda bi, i: (0, 0, 0, 0), pipeline_mode=pl.Buffered(1)),
        ],
        out_specs=pl.BlockSpec((None, tq, ATT_W), lambda bi, i: (bi, i, 0)),
        scratch_shapes=[pltpu.VMEM((2, ATT_W, l), BF16),
                        pltpu.VMEM((2, ATT_UNROLL, N_HEADS // 2, 2 * GRID_W, WIN_ROWS * GRID_W), F32),
                        pltpu.VMEM((2, ATT_UNROLL, N_HEADS // 2, 2 * GRID_W, 128), F32)],
        compiler_params=pltpu.CompilerParams(dimension_semantics=("arbitrary", "arbitrary")),
        name="nbr_attention",
    )(z3, z3, z3, bias_tab)


def _outproj_kernel(x_ref, yl_ref, u_ref, x0_ref, gh_ref, at_ref, ga_ref, hb_ref, onh_ref, ona_ref, w_ref, o_ref):
    yh = x0_ref[...].astype(F32) * (yl_ref[...].astype(F32) + u_ref[...].astype(F32) * hb_ref[...])
    yh = yh * lax.rsqrt(jnp.mean(yh * yh, axis=-1, keepdims=True) + EPS) * onh_ref[...]
    yh = yh * jax.nn.silu(gh_ref[...].astype(F32))
    ya = at_ref[...].astype(F32)
    ya = ya * lax.rsqrt(jnp.mean(ya * ya, axis=-1, keepdims=True) + EPS) * ona_ref[...]
    ya = ya * jax.nn.silu(ga_ref[...].astype(F32))
    acc = _dot(yh.astype(BF16), w_ref[:HY_W, :]) + _dot(ya.astype(BF16), w_ref[HY_W:, :])
    o_ref[...] = x_ref[...] + acc


def _outproj(x2, ylong, u, x0c, z2, att, hy_bias, on_hy, on_att, w_out_bf):
    t = x2.shape[0]
    tm = TOKEN_TILE
    row = lambda c: pl.BlockSpec((tm, CB), lambda i, c=c: (i, c))
    vec = pl.BlockSpec((1, CB), lambda i: (0, 0))
    return pl.pallas_call(
        _outproj_kernel,
        out_shape=jax.ShapeDtypeStruct((t, D_MODEL), F32),
        grid=(t // tm,),
        in_specs=[pl.BlockSpec((tm, D_MODEL), lambda i: (i, 0)), row(0), row(0), row(0), row(Z_GATE_HY), row(0),
                  row(Z_GATE_ATT),
                  vec, vec, vec, pl.BlockSpec((HY_W + ATT_W, D_MODEL), lambda i: (0, 0))],
        out_specs=pl.BlockSpec((tm, D_MODEL), lambda i: (i, 0)),
        compiler_params=pltpu.CompilerParams(dimension_semantics=("arbitrary",)),
        name="outproj",
    )(x2, ylong, u, x0c, z2, att, z2, hy_bias, on_hy, on_att, w_out_bf)


def _dft_tables(l):
    n = 2 * l
    n2 = DFT_N2
    n1 = n // n2
    h = n1 // 2
    k1 = np.arange(n1)
    ang1 = -2.0 * np.pi * ((k1[:, None] * k1[None, :]) % n1) / n1
    f1r, f1i = np.cos(ang1), np.sin(ang1)
    w_data = np.block([[f1r[:, :h], -f1i[:, :h]], [f1i[:, :h], f1r[:, :h]]])
    w_filt = np.concatenate([f1r, f1i], axis=0)
    w_inv = np.block([[f1r[:h], f1i[:h]], [-f1i[:h], f1r[:h]]]) / n
    k2 = np.arange(n2)
    ang2 = -2.0 * np.pi * ((k2[:, None] * k2[None, :]) % n2) / n2
    angt = -2.0 * np.pi * (k1[:, None] * k2[None, :]) / n
    f = lambda a: jnp.asarray(a, dtype=F32)
    return dict(n1=n1, w_data=f(w_data), w_filt=f(w_filt), w_inv=f(w_inv),
                f2r=f(np.cos(ang2)), f2i=f(np.sin(ang2)), twr=f(np.cos(angt)), twi=f(np.sin(angt)))


def _positional_features(l):
    t01 = jnp.linspace(0.0, 1.0, l, dtype=F32)[None, :]
    w = 2.0 * math.pi * jnp.arange(l, dtype=F32)[None, :] / l
    f = jnp.linspace(1e-4, POS_BANDS - 1, POS_BANDS, dtype=F32)[:, None]
    zt = jnp.concatenate([t01, jnp.cos(f * w), -jnp.sin(f * w)], axis=0)
    return jnp.pad(zt, ((0, 128 - zt.shape[0]), (0, 0))), t01.reshape(l, 1)


def _abs_deltas():
    min_decay = math.log(1e-2) / 1.5
    max_decay = math.log(1e-2) / 0.3
    return jnp.abs(jnp.linspace(min_decay, max_decay, 2 * HY_W, dtype=F32))[None, :]


def _bias_table(rpb):
    cols = np.arange(GRID_W)
    cstart = np.clip(cols - WIN_COLS // 2, 0, GRID_W - WIN_COLS)
    kc = np.arange(GRID_W)
    valid = (kc[None, :] >= cstart[:, None]) & (kc[None, :] < cstart[:, None] + WIN_COLS)
    rel = kc[None, :] - cols[:, None] + (WIN_COLS - 1)
    n_rel = 2 * WIN_COLS - 1
    ci, ki = np.nonzero(valid)
    sel = np.zeros((2, n_rel, GRID_W, 2, GRID_W), np.float32)
    for r in range(2):
        sel[r, rel[ci, ki], ci, r, ki] = 1.0
    outside = np.broadcast_to(np.where(valid, 0.0, NEG_INF)[:, None, :], (GRID_W, 2, GRID_W)).astype(np.float32)
    n_off = 2 * WIN_ROWS - 2
    pairs = jnp.stack([rpb[:, :-1], rpb[:, 1:]], axis=2).astype(F32)
    lhs = pairs.transpose(1, 0, 2, 3).reshape(n_off * N_HEADS, 2 * n_rel) * LOG2E
    t = jnp.dot(lhs, jnp.asarray(sel.reshape(2 * n_rel, -1)), precision=lax.Precision.HIGHEST)
    t = t + jnp.asarray(outside.reshape(1, -1))
    return t.reshape(n_off, N_HEADS // 2, 2 * GRID_W, 2 * GRID_W)


def _filter_spectrum(tabs, zpos, absd, f_w1, f_b1, f_fr1, f_w2, f_b2, f_fr2, f_w3):
    n1 = tabs["n1"]
    w1p = jnp.pad(f_w1, ((0, 128 - f_w1.shape[0]), (0, 0)))
    halves = _filter(zpos, w1p, f_b1[None], f_fr1[None], f_w2, f_b2[None], f_fr2[None], f_w3, absd)
    a = _slab_stage(tabs["w_filt"], [h.reshape(1, n1 // 2, DFT_N2, HY_W) for h in halves], BF16)
    return _mid_filter(a.reshape(2, n1, DFT_N2, HY_W), tabs["f2r"], tabs["f2i"], tabs["twr"], tabs["twi"])


def _layer(x, tabs, kf, bias_tab, bd, norm_g, w_in, conv_w, conv_b, hy_bias, qn_g, kn_g, on_hy, on_att, w_out):
    b, l, _ = x.shape
    t = b * l
    n1 = tabs["n1"]
    x2 = x.reshape(t, D_MODEL)
    qgain = jnp.tile(qn_g, N_HEADS)[None] * (HEAD_DIM ** -0.5 * LOG2E)
    kgain = jnp.tile(kn_g, N_HEADS)[None]
    z2, u, x0c = _inproj(x2, l, norm_g[None], w_in.astype(BF16), qgain, kgain, bd, conv_w, conv_b[None])
    z3 = z2.reshape(b, l, z2.shape[1])
    a = _slab_stage(tabs["w_data"], [u.reshape(b // 2, n1, DFT_N2, HY_W)], BF16)
    bp = _mid_data(a.reshape(b // 2, 2, n1, DFT_N2, HY_W), kf, tabs["f2r"], tabs["f2i"], tabs["twr"], tabs["twi"])
    ylong = _slab_stage(tabs["w_inv"], [bp.reshape(b // 2, 2 * n1, DFT_N2, HY_W)], BF16)
    att = _attention(z3, bias_tab)
    out = _outproj(x2, ylong.reshape(t, HY_W), u.reshape(t, HY_W), x0c.reshape(t, HY_W), z2,
                   att.reshape(t, ATT_W), hy_bias[None], on_hy[None], on_att[None], w_out.astype(BF16))
    return out.reshape(b, l, D_MODEL)


def _trunk(x, norm_g, w_in, conv_w, conv_b, f_w1, f_b1, f_fr1, f_w2, f_b2, f_fr2, f_w3,
           hy_bias, qn_g, kn_g, rpb, on_hy, on_att, w_out):
    b, l, _ = x.shape
    assert b % 2 == 0 and l % (GRID_W * ATT_ROWS) == 0 and l // GRID_W >= WIN_ROWS
    assert (2 * l) % (DFT_N2 * MID_KB) == 0 and l % TOKEN_TILE == 0
    tabs = _dft_tables(l)
    zpos = _positional_features(l)
    absd = _abs_deltas()
    head = np.arange(CB) // HEAD_DIM
    bd = jnp.asarray((head[:, None] == head[None, :]).astype(np.float32) / HEAD_DIM, dtype=BF16)
    for i in range(norm_g.shape[0]):
        kf = _filter_spectrum(tabs, zpos, absd, f_w1[i], f_b1[i], f_fr1[i], f_w2[i], f_b2[i], f_fr2[i], f_w3[i])
        x = _layer(x, tabs, kf, _bias_table(rpb[i]), bd, norm_g[i], w_in[i], conv_w[i], conv_b[i], hy_bias[i],
                   qn_g[i], kn_g[i], on_hy[i], on_att[i], w_out[i])
    return x


def kernel(x_prompt, x_sample, norm_g, w_in, conv_w, conv_b, f_w1, f_b1, f_fr1, f_w2, f_b2, f_fr2, f_w3,
           hy_bias, qn_g, kn_g, rpb, on_hy, on_att, w_out):
    params = (norm_g, w_in, conv_w, conv_b, f_w1, f_b1, f_fr1, f_w2, f_b2, f_fr2, f_w3,
              hy_bias, qn_g, kn_g, rpb, on_hy, on_att, w_out)
    return (_trunk(x_prompt, *params), _trunk(x_sample, *params))
```

```python
import functools
import math

import numpy as np
import jax
import jax.numpy as jnp
from jax import lax
from jax.experimental import pallas as pl
from jax.experimental.pallas import tpu as pltpu

F32 = jnp.float32
BF16 = jnp.bfloat16

D_MODEL = 1024
GRID_W = 64
HY_W = 512
ATT_W = 512
HEAD_DIM = 64
N_HEADS = 8
WIN_ROWS = 8
WIN_COLS = 16
POS_BANDS = 16
FILTER_HIDDEN = 64
N_IN = 4 * HY_W + 4 * ATT_W
EPS = 1e-6
NEG_INF = -1e30

CB = 512
DFT_N2 = 256
TOKEN_TILE = 1024
N_HY_SLOTS = 3
Z_GATE_HY, Z_Q, Z_K, Z_V, Z_GATE_ATT = range(5)
ATT_ROWS = 8
KT_UNIT = 128
ATT_UNROLL = 2
LOG2E = math.log2(math.e)
MID_KB = 8
MID_COLS = 256
SUBLANES = 8
SLAB_ROWS = 16
SLAB_TARGET = 128
OUT_SLABS = 64


def _dot(a, b):
    return jnp.dot(a, b, preferred_element_type=F32)


def _dot_hi(a, b):
    return jnp.dot(a, b, preferred_element_type=F32, precision=lax.Precision.HIGHEST)


def _split(x):
    hi = x.astype(BF16)
    lo = (x - hi.astype(F32)).astype(BF16)
    return hi, lo


def _dot3(a, b):
    ah, al = _split(a)
    bh, bl = _split(b)
    return _dot(ah, bh) + (_dot(ah, bl) + _dot(al, bh))


def _inproj_kernel(x_ref, xp_ref, xn_ref, g_ref, w_ref, qg_ref, kg_ref, bd_ref, cw_ref, cb_ref,
                   z_ref, u_ref, x0_ref, *, tiles_per_seq):
    pos = pl.program_id(0) % tiles_per_seq
    tm = x_ref.shape[0]

    def norm(x):
        ms = jnp.mean(x * x, axis=-1, keepdims=True)
        return (x * lax.rsqrt(ms + EPS) * g_ref[...]).astype(BF16)

    h = norm(x_ref[...])
    hp = norm(xp_ref[...])
    hn = norm(xn_ref[...])
    rows = lax.broadcasted_iota(jnp.int32, (tm, CB), 0)
    conv = []
    for j in range(N_HY_SLOTS):
        cols = slice(j * CB, (j + 1) * CB)
        cur = _dot(h, w_ref[:, cols])
        prev_row = jnp.where(pos > 0, _dot(hp, w_ref[:, cols])[SUBLANES - 1:SUBLANES, :], 0.0)
        next_row = jnp.where(pos < tiles_per_seq - 1, _dot(hn, w_ref[:, cols])[0:1, :], 0.0)
        dn = jnp.where(rows == 0, prev_row, pltpu.roll(cur, 1, 0))
        up = jnp.where(rows == tm - 1, next_row, pltpu.roll(cur, tm - 1, 0))
        cw = cw_ref[:, cols]
        conv.append(cw[0:1, :] * dn + cw[1:2, :] * cur + cw[2:3, :] * up + cb_ref[:, cols])
    x0_ref[...] = conv[0].astype(x0_ref.dtype)
    u_ref[...] = (conv[1] * conv[2]).astype(u_ref.dtype)
    for j in range(N_HY_SLOTS, N_IN // CB):
        zj = _dot(h, w_ref[:, j * CB:(j + 1) * CB])
        if j in (4, 5):
            hm = _dot((zj * zj).astype(BF16), bd_ref[...])
            gain = qg_ref[...] if j == 4 else kg_ref[...]
            zj = zj * lax.rsqrt(hm + EPS) * gain
        z_ref[:, (j - N_HY_SLOTS) * CB:(j - N_HY_SLOTS + 1) * CB] = zj.astype(z_ref.dtype)


def _inproj(x2, seq_len, norm_g, w_in_bf, qgain, kgain, bd, conv_w, conv_b):
    t = x2.shape[0]
    tm = TOKEN_TILE
    hb = tm // SUBLANES
    nh = t // SUBLANES
    const = lambda i: (0, 0)
    row = lambda w: pl.BlockSpec((tm, w), lambda i: (i, 0))
    z_w = N_IN - N_HY_SLOTS * CB
    return pl.pallas_call(
        functools.partial(_inproj_kernel, tiles_per_seq=seq_len // tm),
        out_shape=(jax.ShapeDtypeStruct((t, z_w), BF16), jax.ShapeDtypeStruct((t, CB), BF16),
                   jax.ShapeDtypeStruct((t, CB), BF16)),
        grid=(t // tm,),
        in_specs=[
            row(D_MODEL),
            pl.BlockSpec((SUBLANES, D_MODEL), lambda i: (jnp.maximum(i * hb - 1, 0), 0)),
            pl.BlockSpec((SUBLANES, D_MODEL), lambda i: (jnp.minimum((i + 1) * hb, nh - 1), 0)),
            pl.BlockSpec((1, D_MODEL), const),
            pl.BlockSpec((D_MODEL, N_IN), const),
            pl.BlockSpec((1, CB), const),
            pl.BlockSpec((1, CB), const),
            pl.BlockSpec((CB, CB), const),
            pl.BlockSpec((3, N_HY_SLOTS * CB), const),
            pl.BlockSpec((1, N_HY_SLOTS * CB), const),
        ],
        out_specs=(row(z_w), row(CB), row(CB)),
        compiler_params=pltpu.CompilerParams(dimension_semantics=("arbitrary",)),
        name="inproj",
    )(x2, x2, x2, norm_g, w_in_bf, qgain, kgain, bd, conv_w, conv_b)


def _filter_kernel(zt_ref, t_ref, w1t_ref, b1_ref, fr1_ref, w2t_ref, b2_ref, fr2_ref, w3_ref, dl_ref,
                   fwd_ref, bwd_ref, carry_ref):
    tn = fwd_ref.shape[0]

    @pl.when(pl.program_id(0) == 0)
    def _():
        carry_ref[...] = jnp.zeros_like(carry_ref)

    h = jnp.sin(fr1_ref[...] * (_dot_hi(w1t_ref[...], zt_ref[...]) + b1_ref[...]))
    h = jnp.sin(fr2_ref[...] * (_dot_hi(w2t_ref[...], h) + b2_ref[...]))
    h = _dot3(h.T, w3_ref[...])
    h = h * jnp.exp(-t_ref[...] * dl_ref[...])
    fwd_ref[...] = h[:, :HY_W].astype(fwd_ref.dtype)
    hb = h[:, HY_W:].astype(BF16)
    i_out = lax.broadcasted_iota(jnp.int32, (tn, tn), 0)
    i_in = lax.broadcasted_iota(jnp.int32, (tn, tn), 1)
    flip = jnp.where((i_out + i_in == tn) & (i_out > 0), 1.0, 0.0).astype(BF16)
    rev = _dot(flip, hb)
    first_row = lax.broadcasted_iota(jnp.int32, rev.shape, 0) == 0
    bwd_ref[...] = jnp.where(first_row, carry_ref[0:1, :], rev).astype(bwd_ref.dtype)
    carry_ref[...] = hb[:carry_ref.shape[0], :].astype(F32)


def _filter(zpos, w1p, b1, fr1, w2, b2, fr2, w3, absdeltas):
    zpos_t, tcol = zpos
    l = tcol.shape[0]
    tn = min(512, l)
    nb = l // tn
    const = lambda i: (0, 0)
    col = pl.BlockSpec((FILTER_HIDDEN, 1), const)
    half = jax.ShapeDtypeStruct((l, HY_W), BF16)
    return pl.pallas_call(
        _filter_kernel,
        out_shape=(half, half),
        grid=(nb,),
        in_specs=[
            pl.BlockSpec((128, tn), lambda i: (0, nb - 1 - i)),
            pl.BlockSpec((tn, 1), lambda i: (nb - 1 - i, 0)),
            pl.BlockSpec((FILTER_HIDDEN, 128), const),
            col, col,
            pl.BlockSpec((FILTER_HIDDEN, FILTER_HIDDEN), const),
            col, col,
            pl.BlockSpec((FILTER_HIDDEN, 2 * HY_W), const),
            pl.BlockSpec((1, 2 * HY_W), const),
        ],
        out_specs=(pl.BlockSpec((tn, HY_W), lambda i: (nb - 1 - i, 0)), pl.BlockSpec((tn, HY_W), lambda i: (i, 0))),
        scratch_shapes=[pltpu.VMEM((SUBLANES, HY_W), F32)],
        compiler_params=pltpu.CompilerParams(dimension_semantics=("arbitrary",)),
        name="hyena_filter",
    )(zpos_t, tcol, w1p.T, b1.T, fr1.T, w2.T, b2.T, fr2.T, w3, absdeltas)


def _expand_kron(w_ref, wk_scr):
    m, k = w_ref.shape
    m8, k8 = wk_scr.shape
    onehot = lambda cond: jnp.where(cond, 1.0, 0.0).astype(BF16)
    col_exp = onehot(lax.broadcasted_iota(jnp.int32, (k, k8), 1) // SUBLANES
                     == lax.broadcasted_iota(jnp.int32, (k, k8), 0))
    wcols = _dot(w_ref[...].astype(BF16), col_exp).astype(BF16)
    rb = min(m8, 256)
    for r in range(m8 // rb):
        rows = r * rb + lax.broadcasted_iota(jnp.int32, (rb, m), 0)
        row_exp = onehot(rows // SUBLANES == lax.broadcasted_iota(jnp.int32, (rb, m), 1))
        blk = _dot(row_exp, wcols)
        same = (lax.broadcasted_iota(jnp.int32, (rb, k8), 0) % SUBLANES
                == lax.broadcasted_iota(jnp.int32, (rb, k8), 1) % SUBLANES)
        wk_scr[r * rb:(r + 1) * rb, :] = jnp.where(same, blk, 0.0).astype(BF16)


def _slab_kernel(w_ref, *refs):
    x_refs, o_ref, wk_scr = refs[:-2], refs[-2], refs[-1]
    k = sum(r.shape[0] for r in x_refs)
    _, sub, c = x_refs[0].shape
    m = o_ref.shape[0]

    @pl.when((pl.program_id(0) == 0) & (pl.program_id(1) == 0))
    def _():
        _expand_kron(w_ref, wk_scr)

    x = jnp.concatenate([r[...].astype(F32) for r in x_refs], axis=0)
    halves = []
    for h in range(sub // SUBLANES):
        xh = x[:, h * SUBLANES:(h + 1) * SUBLANES, :].reshape(k * SUBLANES, c)
        halves.append(_dot(wk_scr[...], xh.astype(BF16)).reshape(m, SUBLANES, c))
    o_ref[...] = jnp.concatenate(halves, axis=1).astype(o_ref.dtype)


def _slab_stage(w, xs, out_dtype):
    p, _, n2, c = xs[0].shape
    k = sum(x.shape[1] for x in xs)
    m = w.shape[0]
    assert w.shape == (m, k)
    rows = SLAB_ROWS * max(1, SLAB_TARGET // max(m, k))
    return pl.pallas_call(
        _slab_kernel,
        out_shape=jax.ShapeDtypeStruct((p, m, n2, c), out_dtype),
        grid=(p, n2 // rows),
        in_specs=[pl.BlockSpec((m, k), lambda pi, j: (0, 0))]
        + [pl.BlockSpec((None, x.shape[1], rows, c), lambda pi, j: (pi, 0, j, 0)) for x in xs],
        out_specs=pl.BlockSpec((None, m, rows, c), lambda pi, j: (pi, 0, j, 0)),
        scratch_shapes=[pltpu.VMEM((m * SUBLANES, k * SUBLANES), BF16)],
        compiler_params=pltpu.CompilerParams(dimension_semantics=("arbitrary", "arbitrary")),
        name="dft_outer",
    )(w, *xs)


def _stacked_g(f2r, f2i, tr, ti):
    gr = f2r * tr - f2i * ti
    gi = f2r * ti + f2i * tr
    return gr, gi


def _stack(a, b, c, d):
    return jnp.concatenate([jnp.concatenate([a, b], axis=1), jnp.concatenate([c, d], axis=1)], axis=0)


def _mid_data_kernel(a_ref, kf_ref, f2r_ref, f2i_ref, twr_ref, twi_ref, o_ref, gs_scr, gh_scr):
    @pl.when(pl.program_id(1) == 0)
    def _():
        f2r = f2r_ref[...]
        f2i = f2i_ref[...]
        for j in range(MID_KB):
            gr, gi = _stacked_g(f2r, f2i, twr_ref[j:j + 1, :], twi_ref[j:j + 1, :])
            gs_scr[j] = _stack(gr, -gi, gi, gr).astype(BF16)
            grt, git = gr.T, gi.T
            gh_scr[j] = _stack(grt, git, -git, grt).astype(BF16)

    c = a_ref.shape[-1]
    for j in range(MID_KB):
        for cb in range(c // MID_COLS):
            cols = slice(cb * MID_COLS, (cb + 1) * MID_COLS)
            a = jnp.concatenate([a_ref[0, j, :, cols], a_ref[1, j, :, cols]], axis=0)
            x = _dot(gs_scr[j], a)
            xr, xi = x[:DFT_N2], x[DFT_N2:]
            kr, ki = kf_ref[0, j, :, cols].astype(F32), kf_ref[1, j, :, cols].astype(F32)
            y = jnp.concatenate([xr * kr - xi * ki, xr * ki + xi * kr], axis=0).astype(BF16)
            bp = _dot(gh_scr[j], y)
            o_ref[0, j, :, cols] = bp[:DFT_N2].astype(o_ref.dtype)
            o_ref[1, j, :, cols] = bp[DFT_N2:].astype(o_ref.dtype)


def _mid_filter_kernel(a_ref, f2r_ref, f2i_ref, twr_ref, twi_ref, o_ref):
    f2r = f2r_ref[...]
    f2i = f2i_ref[...]
    for j in range(MID_KB):
        gr, gi = _stacked_g(f2r, f2i, twr_ref[j:j + 1, :], twi_ref[j:j + 1, :])
        gs = _stack(gr, -gi, gi, gr).astype(BF16)
        a = jnp.concatenate([a_ref[0, j], a_ref[1, j]], axis=0)
        x = _dot(gs, a)
        o_ref[0, j] = x[:DFT_N2].astype(o_ref.dtype)
        o_ref[1, j] = x[DFT_N2:].astype(o_ref.dtype)


def _mid_specs(n1):
    sq = pl.BlockSpec((DFT_N2, DFT_N2), lambda k, p: (0, 0))
    tw = pl.BlockSpec((MID_KB, DFT_N2), lambda k, p: (k, 0))
    return sq, tw


def _mid_data(a5, kf, f2r, f2i, twr, twi):
    p, _, n1, n2, c = a5.shape
    sq, tw = _mid_specs(n1)
    blk = pl.BlockSpec((None, 2, MID_KB, n2, c), lambda k, pi: (pi, 0, k, 0, 0))
    return pl.pallas_call(
        _mid_data_kernel,
        out_shape=jax.ShapeDtypeStruct(a5.shape, BF16),
        grid=(n1 // MID_KB, p),
        in_specs=[blk, pl.BlockSpec((2, MID_KB, n2, c), lambda k, pi: (0, k, 0, 0)), sq, sq, tw, tw],
        out_specs=blk,
        scratch_shapes=[pltpu.VMEM((MID_KB, 2 * n2, 2 * n2), BF16), pltpu.VMEM((MID_KB, 2 * n2, 2 * n2), BF16)],
        compiler_params=pltpu.CompilerParams(dimension_semantics=("arbitrary", "arbitrary")),
        name="dft_mid",
    )(a5, kf, f2r, f2i, twr, twi)


def _mid_filter(a4, f2r, f2i, twr, twi):
    _, n1, n2, c = a4.shape
    sq, tw = _mid_specs(n1)
    blk = pl.BlockSpec((2, MID_KB, n2, c), lambda k, pi: (0, k, 0, 0))
    return pl.pallas_call(
        _mid_filter_kernel,
        out_shape=jax.ShapeDtypeStruct(a4.shape, BF16),
        grid=(n1 // MID_KB, 1),
        in_specs=[blk, sq, sq, tw, tw],
        out_specs=blk,
        compiler_params=pltpu.CompilerParams(dimension_semantics=("arbitrary", "arbitrary")),
        name="dft_mid_filter",
    )(a4, f2r, f2i, twr, twi)


def _attn_kernel(q_ref, k_ref, v_ref, bias_ref, o_ref, kt_scr, s_scr, m_scr, *, rows):
    rb = pl.program_id(1)
    lane = lax.broadcasted_iota(jnp.int32, (GRID_W, 128), 1)
    first = lane < HEAD_DIM
    band = WIN_ROWS * GRID_W
    ones = jnp.ones((band, 128), BF16)
    n_keys = rows * GRID_W
    units_per_block = ATT_ROWS * GRID_W // KT_UNIT
    n_trips = ATT_ROWS // ATT_UNROLL
    assert units_per_block == n_trips
    pairs = range(N_HEADS // 2)

    def transpose_unit(m):
        lanes = pl.ds(pl.multiple_of(m * KT_UNIT, KT_UNIT), KT_UNIT)
        kt_scr[0, :, lanes] = k_ref[pl.ds(pl.multiple_of(m * KT_UNIT, KT_UNIT), KT_UNIT), :].T
        src = jnp.minimum(m * KT_UNIT + GRID_W, n_keys - KT_UNIT)
        kt_scr[1, :, lanes] = k_ref[pl.ds(pl.multiple_of(src, GRID_W), KT_UNIT), :].T

    @pl.when(rb == 0)
    def _():
        for j in range(units_per_block):
            transpose_unit(j)

    nxt = jnp.minimum(rb + 1, pl.num_programs(1) - 1)

    def row_info(t):
        info = []
        for a in range(ATT_UNROLL):
            rl = t * ATT_UNROLL + a
            r = rb * ATT_ROWS + rl
            r0 = jnp.clip(r - WIN_ROWS // 2, 0, rows - WIN_ROWS)
            par = r0 % 2
            info.append((r - r0, par, (r0 - par) * GRID_W, r0 * GRID_W, pl.multiple_of(rl * GRID_W, GRID_W)))
        return info

    def scores(t, slot):
        for a, (d, par, kt_start, start, qoff) in enumerate(row_info(t)):
            for pr in pairs:
                cs = slice(pr * 128, (pr + 1) * 128)
                q2 = q_ref[pl.ds(qoff, GRID_W), cs]
                zero = jnp.zeros_like(q2)
                qbd = jnp.concatenate([jnp.where(first, q2, zero), jnp.where(first, zero, q2)], axis=0)
                bias = [bias_ref[2 * j + (WIN_ROWS - 1) - d, pr] for j in range(band // KT_UNIT)]
                kl = pl.ds(pl.multiple_of(kt_start, KT_UNIT), band)
                s = _dot(qbd, kt_scr[par, cs, kl]) + jnp.concatenate(bias, axis=1)
                s_scr[slot, a, pr] = s
                m_scr[slot, a, pr] = jnp.broadcast_to(jnp.max(s, axis=-1, keepdims=True), (2 * GRID_W, 128))

    def outputs(t, slot):
        for a, (d, par, kt_start, start, qoff) in enumerate(row_info(t)):
            for pr in pairs:
                cs = slice(pr * 128, (pr + 1) * 128)
                m = m_scr[slot, a, pr]
                p = jnp.exp2(s_scr[slot, a, pr] - jnp.concatenate([m] * (band // 128), axis=1)).astype(BF16)
                ks = pl.ds(pl.multiple_of(start, GRID_W), band)
                acc = _dot(p, jnp.concatenate([v_ref[ks, cs], ones], axis=1))
                o = acc[:, :128] / acc[:, 128:]
                o_ref[pl.ds(qoff, GRID_W), cs] = jnp.where(first, o[:GRID_W], o[GRID_W:]).astype(o_ref.dtype)

    scores(0, 0)
    for t in range(n_trips):
        transpose_unit(nxt * units_per_block + t)
        if t + 1 < n_trips:
            scores(t + 1, (t + 1) % 2)
        outputs(t, t % 2)


def _attention(z3, bias_tab):
    b, l, _ = z3.shape
    rows = l // GRID_W
    tq = ATT_ROWS * GRID_W
    return pl.pallas_call(
        functools.partial(_attn_kernel, rows=rows),
        out_shape=jax.ShapeDtypeStruct((b, l, ATT_W), BF16),
        grid=(b, rows // ATT_ROWS),
        in_specs=[
            pl.BlockSpec((None, tq, CB), lambda bi, i: (bi, i, Z_Q)),
            pl.BlockSpec((None, l, CB), lambda bi, i: (bi, 0, Z_K), pipeline_mode=pl.Buffered(1)),
            pl.BlockSpec((None, l, CB), lambda bi, i: (bi, 0, Z_V)),
            pl.BlockSpec(bias_tab.shape, lambda bi, i: (0, 0, 0, 0), pipeline_mode=pl.Buffered(1)),
        ],
        out_specs=pl.BlockSpec((None, tq, ATT_W), lambda bi, i: (bi, i, 0)),
        scratch_shapes=[pltpu.VMEM((2, ATT_W, l), BF16),
                        pltpu.VMEM((2, ATT_UNROLL, N_HEADS // 2, 2 * GRID_W, WIN_ROWS * GRID_W), F32),
                        pltpu.VMEM((2, ATT_UNROLL, N_HEADS // 2, 2 * GRID_W, 128), F32)],
        compiler_params=pltpu.CompilerParams(dimension_semantics=("arbitrary", "arbitrary")),
        name="nbr_attention",
    )(z3, z3, z3, bias_tab)


def _outproj_kernel(winv_ref, bp_ref, x_ref, u_ref, x0_ref, gh_ref, at_ref, ga_ref, hb_ref, onh_ref, ona_ref,
                    w_ref, o_ref, wk_scr):
    @pl.when((pl.program_id(0) == 0) & (pl.program_id(1) == 0))
    def _():
        _expand_kron(winv_ref, wk_scr)

    n1, rows, _ = x_ref.shape
    k = bp_ref.shape[0]
    tok = n1 * SUBLANES
    load = lambda ref: ref[...].astype(F32)
    bp, u, x0, gh, at, ga = (load(r) for r in (bp_ref, u_ref, x0_ref, gh_ref, at_ref, ga_ref))
    for h in range(rows // SUBLANES):
        sl = slice(h * SUBLANES, (h + 1) * SUBLANES)
        part = lambda v: v[:, sl, :].reshape(tok, v.shape[-1])
        ylong = _dot(wk_scr[...], bp[:, sl, :].reshape(k * SUBLANES, HY_W).astype(BF16))
        yh = part(x0) * (ylong + part(u) * hb_ref[...])
        yh = yh * lax.rsqrt(jnp.mean(yh * yh, axis=-1, keepdims=True) + EPS) * onh_ref[...]
        yh = yh * jax.nn.silu(part(gh))
        ya = part(at)
        ya = ya * lax.rsqrt(jnp.mean(ya * ya, axis=-1, keepdims=True) + EPS) * ona_ref[...]
        ya = ya * jax.nn.silu(part(ga))
        acc = _dot(yh.astype(BF16), w_ref[:HY_W, :]) + _dot(ya.astype(BF16), w_ref[HY_W:, :])
        o_ref[:, sl, :] = (x_ref[:, sl, :].reshape(tok, D_MODEL) + acc).reshape(n1, SUBLANES, D_MODEL)


def _outproj(w_inv, bp, x, u, x0c, z, att, hy_bias, on_hy, on_att, w_out_bf):
    p, n1, n2, _ = x.shape
    rows = SLAB_ROWS * max(1, OUT_SLABS // n1)
    blk = lambda width, c: pl.BlockSpec((None, n1, rows, width), lambda pi, j, c=c: (pi, 0, j, c))
    vec = pl.BlockSpec((1, CB), lambda pi, j: (0, 0))
    return pl.pallas_call(
        _outproj_kernel,
        out_shape=jax.ShapeDtypeStruct(x.shape, F32),
        grid=(p, n2 // rows),
        in_specs=[pl.BlockSpec(w_inv.shape, lambda pi, j: (0, 0)),
                  pl.BlockSpec((None, 2 * n1, rows, CB), lambda pi, j: (pi, 0, j, 0)),
                  blk(D_MODEL, 0), blk(CB, 0), blk(CB, 0), blk(CB, Z_GATE_HY), blk(CB, 0), blk(CB, Z_GATE_ATT),
                  vec, vec, vec, pl.BlockSpec((HY_W + ATT_W, D_MODEL), lambda pi, j: (0, 0))],
        out_specs=blk(D_MODEL, 0),
        scratch_shapes=[pltpu.VMEM((n1 * SUBLANES, 2 * n1 * SUBLANES), BF16)],
        compiler_params=pltpu.CompilerParams(dimension_semantics=("arbitrary", "arbitrary")),
        name="outproj",
    )(w_inv, bp, x, u, x0c, z, att, z, hy_bias, on_hy, on_att, w_out_bf)


def _dft_tables(l):
    n = 2 * l
    n2 = DFT_N2
    n1 = n // n2
    h = n1 // 2
    k1 = np.arange(n1)
    ang1 = -2.0 * np.pi * ((k1[:, None] * k1[None, :]) % n1) / n1
    f1r, f1i = np.cos(ang1), np.sin(ang1)
    w_data = np.block([[f1r[:, :h], -f1i[:, :h]], [f1i[:, :h], f1r[:, :h]]])
    w_filt = np.concatenate([f1r, f1i], axis=0)
    w_inv = np.block([[f1r[:h], f1i[:h]], [-f1i[:h], f1r[:h]]]) / n
    k2 = np.arange(n2)
    ang2 = -2.0 * np.pi * ((k2[:, None] * k2[None, :]) % n2) / n2
    angt = -2.0 * np.pi * (k1[:, None] * k2[None, :]) / n
    f = lambda a: jnp.asarray(a, dtype=F32)
    return dict(n1=n1, w_data=f(w_data), w_filt=f(w_filt), w_inv=f(w_inv),
                f2r=f(np.cos(ang2)), f2i=f(np.sin(ang2)), twr=f(np.cos(angt)), twi=f(np.sin(angt)))


def _positional_features(l):
    t01 = jnp.linspace(0.0, 1.0, l, dtype=F32)[None, :]
    w = 2.0 * math.pi * jnp.arange(l, dtype=F32)[None, :] / l
    f = jnp.linspace(1e-4, POS_BANDS - 1, POS_BANDS, dtype=F32)[:, None]
    zt = jnp.concatenate([t01, jnp.cos(f * w), -jnp.sin(f * w)], axis=0)
    return jnp.pad(zt, ((0, 128 - zt.shape[0]), (0, 0))), t01.reshape(l, 1)


def _abs_deltas():
    min_decay = math.log(1e-2) / 1.5
    max_decay = math.log(1e-2) / 0.3
    return jnp.abs(jnp.linspace(min_decay, max_decay, 2 * HY_W, dtype=F32))[None, :]


def _bias_table(rpb):
    cols = np.arange(GRID_W)
    cstart = np.clip(cols - WIN_COLS // 2, 0, GRID_W - WIN_COLS)
    kc = np.arange(GRID_W)
    valid = (kc[None, :] >= cstart[:, None]) & (kc[None, :] < cstart[:, None] + WIN_COLS)
    rel = kc[None, :] - cols[:, None] + (WIN_COLS - 1)
    n_rel = 2 * WIN_COLS - 1
    ci, ki = np.nonzero(valid)
    sel = np.zeros((2, n_rel, GRID_W, 2, GRID_W), np.float32)
    for r in range(2):
        sel[r, rel[ci, ki], ci, r, ki] = 1.0
    outside = np.broadcast_to(np.where(valid, 0.0, NEG_INF)[:, None, :], (GRID_W, 2, GRID_W)).astype(np.float32)
    n_off = 2 * WIN_ROWS - 2
    pairs = jnp.stack([rpb[:, :-1], rpb[:, 1:]], axis=2).astype(F32)
    lhs = pairs.transpose(1, 0, 2, 3).reshape(n_off * N_HEADS, 2 * n_rel) * LOG2E
    t = jnp.dot(lhs, jnp.asarray(sel.reshape(2 * n_rel, -1)), precision=lax.Precision.HIGHEST)
    t = t + jnp.asarray(outside.reshape(1, -1))
    return t.reshape(n_off, N_HEADS // 2, 2 * GRID_W, 2 * GRID_W)


def _filter_spectrum(tabs, zpos, absd, f_w1, f_b1, f_fr1, f_w2, f_b2, f_fr2, f_w3):
    n1 = tabs["n1"]
    w1p = jnp.pad(f_w1, ((0, 128 - f_w1.shape[0]), (0, 0)))
    halves = _filter(zpos, w1p, f_b1[None], f_fr1[None], f_w2, f_b2[None], f_fr2[None], f_w3, absd)
    a = _slab_stage(tabs["w_filt"], [h.reshape(1, n1 // 2, DFT_N2, HY_W) for h in halves], BF16)
    return _mid_filter(a.reshape(2, n1, DFT_N2, HY_W), tabs["f2r"], tabs["f2i"], tabs["twr"], tabs["twi"])


def _layer(x, tabs, kf, bias_tab, bd, norm_g, w_in, conv_w, conv_b, hy_bias, qn_g, kn_g, on_hy, on_att, w_out):
    b, l, _ = x.shape
    t = b * l
    n1 = tabs["n1"]
    x2 = x.reshape(t, D_MODEL)
    qgain = jnp.tile(qn_g, N_HEADS)[None] * (HEAD_DIM ** -0.5 * LOG2E)
    kgain = jnp.tile(kn_g, N_HEADS)[None]
    z2, u, x0c = _inproj(x2, l, norm_g[None], w_in.astype(BF16), qgain, kgain, bd, conv_w, conv_b[None])
    z3 = z2.reshape(b, l, z2.shape[1])
    a = _slab_stage(tabs["w_data"], [u.reshape(b // 2, n1, DFT_N2, HY_W)], BF16)
    bp = _mid_data(a.reshape(b // 2, 2, n1, DFT_N2, HY_W), kf, tabs["f2r"], tabs["f2i"], tabs["twr"], tabs["twi"])
    att = _attention(z3, bias_tab)
    view = lambda v: v.reshape(b // 2, n1, DFT_N2, v.shape[-1])
    out = _outproj(tabs["w_inv"], bp.reshape(b // 2, 2 * n1, DFT_N2, HY_W), view(x), view(u), view(x0c), view(z2),
                   view(att), hy_bias[None], on_hy[None], on_att[None], w_out.astype(BF16))
    return out.reshape(b, l, D_MODEL)


def _trunk(x, norm_g, w_in, conv_w, conv_b, f_w1, f_b1, f_fr1, f_w2, f_b2, f_fr2, f_w3,
           hy_bias, qn_g, kn_g, rpb, on_hy, on_att, w_out):
    b, l, _ = x.shape
    assert b % 2 == 0 and l % (GRID_W * ATT_ROWS) == 0 and l // GRID_W >= WIN_ROWS
    assert (2 * l) % (DFT_N2 * MID_KB) == 0 and l % TOKEN_TILE == 0
    tabs = _dft_tables(l)
    zpos = _positional_features(l)
    absd = _abs_deltas()
    head = np.arange(CB) // HEAD_DIM
    bd = jnp.asarray((head[:, None] == head[None, :]).astype(np.float32) / HEAD_DIM, dtype=BF16)
    for i in range(norm_g.shape[0]):
        kf = _filter_spectrum(tabs, zpos, absd, f_w1[i], f_b1[i], f_fr1[i], f_w2[i], f_b2[i], f_fr2[i], f_w3[i])
        x = _layer(x, tabs, kf, _bias_table(rpb[i]), bd, norm_g[i], w_in[i], conv_w[i], conv_b[i], hy_bias[i],
                   qn_g[i], kn_g[i], on_hy[i], on_att[i], w_out[i])
    return x


def kernel(x_prompt, x_sample, norm_g, w_in, conv_w, conv_b, f_w1, f_b1, f_fr1, f_w2, f_b2, f_fr2, f_w3,
           hy_bias, qn_g, kn_g, rpb, on_hy, on_att, w_out):
    params = (norm_g, w_in, conv_w, conv_b, f_w1, f_b1, f_fr1, f_w2, f_b2, f_fr2, f_w3,
              hy_bias, qn_g, kn_g, rpb, on_hy, on_att, w_out)
    return (_trunk(x_prompt, *params), _trunk(x_sample, *params))
```

```python
import functools
import math

import numpy as np
import jax
import jax.numpy as jnp
from jax import lax
from jax.experimental import pallas as pl
from jax.experimental.pallas import tpu as pltpu

F32 = jnp.float32
BF16 = jnp.bfloat16

D_MODEL = 1024
GRID_W = 64
HY_W = 512
ATT_W = 512
HEAD_DIM = 64
N_HEADS = 8
WIN_ROWS = 8
WIN_COLS = 16
POS_BANDS = 16
FILTER_HIDDEN = 64
N_IN = 4 * HY_W + 4 * ATT_W
EPS = 1e-6
NEG_INF = -1e30

CB = 512
DFT_N2 = 256
TOKEN_TILE = 1024
N_HY_SLOTS = 3
Z_GATE_HY, Z_Q, Z_K, Z_V, Z_GATE_ATT = range(5)
ATT_ROWS = 8
KT_UNIT = 128
ATT_UNROLL = 2
LOG2E = math.log2(math.e)
MID_KB = 8
MID_COLS = 256
SUBLANES = 8
SLAB_ROWS = 16
SLAB_TARGET = 256
OUT_SLABS = 64


def _dot(a, b):
    return jnp.dot(a, b, preferred_element_type=F32)


def _dot_hi(a, b):
    return jnp.dot(a, b, preferred_element_type=F32, precision=lax.Precision.HIGHEST)


def _split(x):
    hi = x.astype(BF16)
    lo = (x - hi.astype(F32)).astype(BF16)
    return hi, lo


def _dot3(a, b):
    ah, al = _split(a)
    bh, bl = _split(b)
    return _dot(ah, bh) + (_dot(ah, bl) + _dot(al, bh))


def _inproj_kernel(x_ref, xp_ref, xn_ref, g_ref, w_ref, qg_ref, kg_ref, bd_ref, cw_ref, cb_ref,
                   z_ref, u_ref, x0_ref, *, tiles_per_seq):
    pos = pl.program_id(0) % tiles_per_seq
    tm = x_ref.shape[0]

    def norm(x):
        ms = jnp.mean(x * x, axis=-1, keepdims=True)
        return (x * lax.rsqrt(ms + EPS) * g_ref[...]).astype(BF16)

    h = norm(x_ref[...])
    hp = norm(xp_ref[...])
    hn = norm(xn_ref[...])
    rows = lax.broadcasted_iota(jnp.int32, (tm, CB), 0)
    conv = []
    for j in range(N_HY_SLOTS):
        cols = slice(j * CB, (j + 1) * CB)
        cur = _dot(h, w_ref[:, cols])
        prev_row = jnp.where(pos > 0, _dot(hp, w_ref[:, cols])[SUBLANES - 1:SUBLANES, :], 0.0)
        next_row = jnp.where(pos < tiles_per_seq - 1, _dot(hn, w_ref[:, cols])[0:1, :], 0.0)
        dn = jnp.where(rows == 0, prev_row, pltpu.roll(cur, 1, 0))
        up = jnp.where(rows == tm - 1, next_row, pltpu.roll(cur, tm - 1, 0))
        cw = cw_ref[:, cols]
        conv.append(cw[0:1, :] * dn + cw[1:2, :] * cur + cw[2:3, :] * up + cb_ref[:, cols])
    x0_ref[...] = conv[0].astype(x0_ref.dtype)
    u_ref[...] = (conv[1] * conv[2]).astype(u_ref.dtype)
    for j in range(N_HY_SLOTS, N_IN // CB):
        zj = _dot(h, w_ref[:, j * CB:(j + 1) * CB])
        if j in (4, 5):
            hm = _dot((zj * zj).astype(BF16), bd_ref[...])
            gain = qg_ref[...] if j == 4 else kg_ref[...]
            zj = zj * lax.rsqrt(hm + EPS) * gain
        z_ref[:, (j - N_HY_SLOTS) * CB:(j - N_HY_SLOTS + 1) * CB] = zj.astype(z_ref.dtype)


def _inproj(x2, seq_len, norm_g, w_in_bf, qgain, kgain, bd, conv_w, conv_b):
    t = x2.shape[0]
    tm = TOKEN_TILE
    hb = tm // SUBLANES
    nh = t // SUBLANES
    const = lambda i: (0, 0)
    row = lambda w: pl.BlockSpec((tm, w), lambda i: (i, 0))
    z_w = N_IN - N_HY_SLOTS * CB
    return pl.pallas_call(
        functools.partial(_inproj_kernel, tiles_per_seq=seq_len // tm),
        out_shape=(jax.ShapeDtypeStruct((t, z_w), BF16), jax.ShapeDtypeStruct((t, CB), BF16),
                   jax.ShapeDtypeStruct((t, CB), BF16)),
        grid=(t // tm,),
        in_specs=[
            row(D_MODEL),
            pl.BlockSpec((SUBLANES, D_MODEL), lambda i: (jnp.maximum(i * hb - 1, 0), 0)),
            pl.BlockSpec((SUBLANES, D_MODEL), lambda i: (jnp.minimum((i + 1) * hb, nh - 1), 0)),
            pl.BlockSpec((1, D_MODEL), const),
            pl.BlockSpec((D_MODEL, N_IN), const),
            pl.BlockSpec((1, CB), const),
            pl.BlockSpec((1, CB), const),
            pl.BlockSpec((CB, CB), const),
            pl.BlockSpec((3, N_HY_SLOTS * CB), const),
            pl.BlockSpec((1, N_HY_SLOTS * CB), const),
        ],
        out_specs=(row(z_w), row(CB), row(CB)),
        compiler_params=pltpu.CompilerParams(dimension_semantics=("arbitrary",)),
        name="inproj",
    )(x2, x2, x2, norm_g, w_in_bf, qgain, kgain, bd, conv_w, conv_b)


def _filter_kernel(zt_ref, t_ref, w1t_ref, b1_ref, fr1_ref, w2t_ref, b2_ref, fr2_ref, w3_ref, dl_ref,
                   fwd_ref, bwd_ref, carry_ref):
    tn = fwd_ref.shape[0]

    @pl.when(pl.program_id(0) == 0)
    def _():
        carry_ref[...] = jnp.zeros_like(carry_ref)

    h = jnp.sin(fr1_ref[...] * (_dot_hi(w1t_ref[...], zt_ref[...]) + b1_ref[...]))
    h = jnp.sin(fr2_ref[...] * (_dot_hi(w2t_ref[...], h) + b2_ref[...]))
    h = _dot3(h.T, w3_ref[...])
    h = h * jnp.exp(-t_ref[...] * dl_ref[...])
    fwd_ref[...] = h[:, :HY_W].astype(fwd_ref.dtype)
    hb = h[:, HY_W:].astype(BF16)
    i_out = lax.broadcasted_iota(jnp.int32, (tn, tn), 0)
    i_in = lax.broadcasted_iota(jnp.int32, (tn, tn), 1)
    flip = jnp.where((i_out + i_in == tn) & (i_out > 0), 1.0, 0.0).astype(BF16)
    rev = _dot(flip, hb)
    first_row = lax.broadcasted_iota(jnp.int32, rev.shape, 0) == 0
    bwd_ref[...] = jnp.where(first_row, carry_ref[0:1, :], rev).astype(bwd_ref.dtype)
    carry_ref[...] = hb[:carry_ref.shape[0], :].astype(F32)


def _filter(zpos, w1p, b1, fr1, w2, b2, fr2, w3, absdeltas):
    zpos_t, tcol = zpos
    l = tcol.shape[0]
    tn = min(512, l)
    nb = l // tn
    const = lambda i: (0, 0)
    col = pl.BlockSpec((FILTER_HIDDEN, 1), const)
    half = jax.ShapeDtypeStruct((l, HY_W), BF16)
    return pl.pallas_call(
        _filter_kernel,
        out_shape=(half, half),
        grid=(nb,),
        in_specs=[
            pl.BlockSpec((128, tn), lambda i: (0, nb - 1 - i)),
            pl.BlockSpec((tn, 1), lambda i: (nb - 1 - i, 0)),
            pl.BlockSpec((FILTER_HIDDEN, 128), const),
            col, col,
            pl.BlockSpec((FILTER_HIDDEN, FILTER_HIDDEN), const),
            col, col,
            pl.BlockSpec((FILTER_HIDDEN, 2 * HY_W), const),
            pl.BlockSpec((1, 2 * HY_W), const),
        ],
        out_specs=(pl.BlockSpec((tn, HY_W), lambda i: (nb - 1 - i, 0)), pl.BlockSpec((tn, HY_W), lambda i: (i, 0))),
        scratch_shapes=[pltpu.VMEM((SUBLANES, HY_W), F32)],
        compiler_params=pltpu.CompilerParams(dimension_semantics=("arbitrary",)),
        name="hyena_filter",
    )(zpos_t, tcol, w1p.T, b1.T, fr1.T, w2.T, b2.T, fr2.T, w3, absdeltas)


def _expand_kron(w_ref, wk_scr):
    m, k = w_ref.shape
    m8, k8 = wk_scr.shape
    onehot = lambda cond: jnp.where(cond, 1.0, 0.0).astype(BF16)
    col_exp = onehot(lax.broadcasted_iota(jnp.int32, (k, k8), 1) // SUBLANES
                     == lax.broadcasted_iota(jnp.int32, (k, k8), 0))
    wcols = _dot(w_ref[...].astype(BF16), col_exp).astype(BF16)
    rb = min(m8, 256)
    for r in range(m8 // rb):
        rows = r * rb + lax.broadcasted_iota(jnp.int32, (rb, m), 0)
        row_exp = onehot(rows // SUBLANES == lax.broadcasted_iota(jnp.int32, (rb, m), 1))
        blk = _dot(row_exp, wcols)
        same = (lax.broadcasted_iota(jnp.int32, (rb, k8), 0) % SUBLANES
                == lax.broadcasted_iota(jnp.int32, (rb, k8), 1) % SUBLANES)
        wk_scr[r * rb:(r + 1) * rb, :] = jnp.where(same, blk, 0.0).astype(BF16)


def _slab_kernel(w_ref, *refs):
    x_refs, o_ref, wk_scr = refs[:-2], refs[-2], refs[-1]
    k = sum(r.shape[0] for r in x_refs)
    _, sub, c = x_refs[0].shape
    m = o_ref.shape[0]

    @pl.when((pl.program_id(0) == 0) & (pl.program_id(1) == 0))
    def _():
        _expand_kron(w_ref, wk_scr)

    x = jnp.concatenate([r[...].astype(F32) for r in x_refs], axis=0)
    halves = []
    for h in range(sub // SUBLANES):
        xh = x[:, h * SUBLANES:(h + 1) * SUBLANES, :].reshape(k * SUBLANES, c)
        halves.append(_dot(wk_scr[...], xh.astype(BF16)).reshape(m, SUBLANES, c))
    o_ref[...] = jnp.concatenate(halves, axis=1).astype(o_ref.dtype)


def _slab_stage(w, xs, out_dtype):
    p, _, n2, c = xs[0].shape
    k = sum(x.shape[1] for x in xs)
    m = w.shape[0]
    assert w.shape == (m, k)
    rows = SLAB_ROWS * max(1, SLAB_TARGET // max(m, k))
    return pl.pallas_call(
        _slab_kernel,
        out_shape=jax.ShapeDtypeStruct((p, m, n2, c), out_dtype),
        grid=(p, n2 // rows),
        in_specs=[pl.BlockSpec((m, k), lambda pi, j: (0, 0))]
        + [pl.BlockSpec((None, x.shape[1], rows, c), lambda pi, j: (pi, 0, j, 0)) for x in xs],
        out_specs=pl.BlockSpec((None, m, rows, c), lambda pi, j: (pi, 0, j, 0)),
        scratch_shapes=[pltpu.VMEM((m * SUBLANES, k * SUBLANES), BF16)],
        compiler_params=pltpu.CompilerParams(dimension_semantics=("arbitrary", "arbitrary")),
        name="dft_outer",
    )(w, *xs)


def _stacked_g(f2r, f2i, tr, ti):
    gr = f2r * tr - f2i * ti
    gi = f2r * ti + f2i * tr
    return gr, gi


def _stack(a, b, c, d):
    return jnp.concatenate([jnp.concatenate([a, b], axis=1), jnp.concatenate([c, d], axis=1)], axis=0)


def _mid_data_kernel(a_ref, kf_ref, f2r_ref, f2i_ref, twr_ref, twi_ref, o_ref, gs_scr, gh_scr):
    @pl.when(pl.program_id(1) == 0)
    def _():
        f2r = f2r_ref[...]
        f2i = f2i_ref[...]
        for j in range(MID_KB):
            gr, gi = _stacked_g(f2r, f2i, twr_ref[j:j + 1, :], twi_ref[j:j + 1, :])
            gs_scr[j] = _stack(gr, -gi, gi, gr).astype(BF16)
            grt, git = gr.T, gi.T
            gh_scr[j] = _stack(grt, git, -git, grt).astype(BF16)

    c = a_ref.shape[-1]
    for j in range(MID_KB):
        for cb in range(c // MID_COLS):
            cols = slice(cb * MID_COLS, (cb + 1) * MID_COLS)
            a = jnp.concatenate([a_ref[0, j, :, cols], a_ref[1, j, :, cols]], axis=0)
            x = _dot(gs_scr[j], a)
            xr, xi = x[:DFT_N2], x[DFT_N2:]
            kr, ki = kf_ref[0, j, :, cols].astype(F32), kf_ref[1, j, :, cols].astype(F32)
            y = jnp.concatenate([xr * kr - xi * ki, xr * ki + xi * kr], axis=0).astype(BF16)
            bp = _dot(gh_scr[j], y)
            o_ref[0, j, :, cols] = bp[:DFT_N2].astype(o_ref.dtype)
            o_ref[1, j, :, cols] = bp[DFT_N2:].astype(o_ref.dtype)


def _mid_filter_kernel(a_ref, f2r_ref, f2i_ref, twr_ref, twi_ref, o_ref):
    f2r = f2r_ref[...]
    f2i = f2i_ref[...]
    for j in range(MID_KB):
        gr, gi = _stacked_g(f2r, f2i, twr_ref[j:j + 1, :], twi_ref[j:j + 1, :])
        gs = _stack(gr, -gi, gi, gr).astype(BF16)
        a = jnp.concatenate([a_ref[0, j], a_ref[1, j]], axis=0)
        x = _dot(gs, a)
        o_ref[0, j] = x[:DFT_N2].astype(o_ref.dtype)
        o_ref[1, j] = x[DFT_N2:].astype(o_ref.dtype)


def _mid_specs(n1):
    sq = pl.BlockSpec((DFT_N2, DFT_N2), lambda k, p: (0, 0))
    tw = pl.BlockSpec((MID_KB, DFT_N2), lambda k, p: (k, 0))
    return sq, tw


def _mid_data(a5, kf, f2r, f2i, twr, twi):
    p, _, n1, n2, c = a5.shape
    sq, tw = _mid_specs(n1)
    blk = pl.BlockSpec((None, 2, MID_KB, n2, c), lambda k, pi: (pi, 0, k, 0, 0))
    return pl.pallas_call(
        _mid_data_kernel,
        out_shape=jax.ShapeDtypeStruct(a5.shape, BF16),
        grid=(n1 // MID_KB, p),
        in_specs=[blk, pl.BlockSpec((2, MID_KB, n2, c), lambda k, pi: (0, k, 0, 0)), sq, sq, tw, tw],
        out_specs=blk,
        scratch_shapes=[pltpu.VMEM((MID_KB, 2 * n2, 2 * n2), BF16), pltpu.VMEM((MID_KB, 2 * n2, 2 * n2), BF16)],
        compiler_params=pltpu.CompilerParams(dimension_semantics=("arbitrary", "arbitrary")),
        name="dft_mid",
    )(a5, kf, f2r, f2i, twr, twi)


def _mid_filter(a4, f2r, f2i, twr, twi):
    _, n1, n2, c = a4.shape
    sq, tw = _mid_specs(n1)
    blk = pl.BlockSpec((2, MID_KB, n2, c), lambda k, pi: (0, k, 0, 0))
    return pl.pallas_call(
        _mid_filter_kernel,
        out_shape=jax.ShapeDtypeStruct(a4.shape, BF16),
        grid=(n1 // MID_KB, 1),
        in_specs=[blk, sq, sq, tw, tw],
        out_specs=blk,
        compiler_params=pltpu.CompilerParams(dimension_semantics=("arbitrary", "arbitrary")),
        name="dft_mid_filter",
    )(a4, f2r, f2i, twr, twi)


def _attn_kernel(q_ref, k_ref, v_ref, bias_ref, o_ref, kt_scr, s_scr, m_scr, *, rows):
    rb = pl.program_id(1)
    lane = lax.broadcasted_iota(jnp.int32, (GRID_W, 128), 1)
    first = lane < HEAD_DIM
    band = WIN_ROWS * GRID_W
    ones = jnp.ones((band, 128), BF16)
    n_keys = rows * GRID_W
    units_per_block = ATT_ROWS * GRID_W // KT_UNIT
    n_trips = ATT_ROWS // ATT_UNROLL
    assert units_per_block == n_trips
    pairs = range(N_HEADS // 2)

    def transpose_unit(m):
        lanes = pl.ds(pl.multiple_of(m * KT_UNIT, KT_UNIT), KT_UNIT)
        kt_scr[0, :, lanes] = k_ref[pl.ds(pl.multiple_of(m * KT_UNIT, KT_UNIT), KT_UNIT), :].T
        src = jnp.minimum(m * KT_UNIT + GRID_W, n_keys - KT_UNIT)
        kt_scr[1, :, lanes] = k_ref[pl.ds(pl.multiple_of(src, GRID_W), KT_UNIT), :].T

    @pl.when(rb == 0)
    def _():
        for j in range(units_per_block):
            transpose_unit(j)

    nxt = jnp.minimum(rb + 1, pl.num_programs(1) - 1)

    def row_info(t):
        info = []
        for a in range(ATT_UNROLL):
            rl = t * ATT_UNROLL + a
            r = rb * ATT_ROWS + rl
            r0 = jnp.clip(r - WIN_ROWS // 2, 0, rows - WIN_ROWS)
            par = r0 % 2
            info.append((r - r0, par, (r0 - par) * GRID_W, r0 * GRID_W, pl.multiple_of(rl * GRID_W, GRID_W)))
        return info

    def scores(t, slot):
        for a, (d, par, kt_start, start, qoff) in enumerate(row_info(t)):
            for pr in pairs:
                cs = slice(pr * 128, (pr + 1) * 128)
                q2 = q_ref[pl.ds(qoff, GRID_W), cs]
                zero = jnp.zeros_like(q2)
                qbd = jnp.concatenate([jnp.where(first, q2, zero), jnp.where(first, zero, q2)], axis=0)
                bias = [bias_ref[2 * j + (WIN_ROWS - 1) - d, pr] for j in range(band // KT_UNIT)]
                kl = pl.ds(pl.multiple_of(kt_start, KT_UNIT), band)
                s = _dot(qbd, kt_scr[par, cs, kl]) + jnp.concatenate(bias, axis=1)
                s_scr[slot, a, pr] = s
                m_scr[slot, a, pr] = jnp.broadcast_to(jnp.max(s, axis=-1, keepdims=True), (2 * GRID_W, 128))

    def outputs(t, slot):
        for a, (d, par, kt_start, start, qoff) in enumerate(row_info(t)):
            for pr in pairs:
                cs = slice(pr * 128, (pr + 1) * 128)
                m = m_scr[slot, a, pr]
                p = jnp.exp2(s_scr[slot, a, pr] - jnp.concatenate([m] * (band // 128), axis=1)).astype(BF16)
                ks = pl.ds(pl.multiple_of(start, GRID_W), band)
                acc = _dot(p, jnp.concatenate([v_ref[ks, cs], ones], axis=1))
                o = acc[:, :128] / acc[:, 128:]
                o_ref[pl.ds(qoff, GRID_W), cs] = jnp.where(first, o[:GRID_W], o[GRID_W:]).astype(o_ref.dtype)

    scores(0, 0)
    for t in range(n_trips):
        transpose_unit(nxt * units_per_block + t)
        if t + 1 < n_trips:
            scores(t + 1, (t + 1) % 2)
        outputs(t, t % 2)


def _attention(z3, bias_tab):
    b, l, _ = z3.shape
    rows = l // GRID_W
    tq = ATT_ROWS * GRID_W
    return pl.pallas_call(
        functools.partial(_attn_kernel, rows=rows),
        out_shape=jax.ShapeDtypeStruct((b, l, ATT_W), BF16),
        grid=(b, rows // ATT_ROWS),
        in_specs=[
            pl.BlockSpec((None, tq, CB), lambda bi, i: (bi, i, Z_Q)),
            pl.BlockSpec((None, l, CB), lambda bi, i: (bi, 0, Z_K), pipeline_mode=pl.Buffered(1)),
            pl.BlockSpec((None, l, CB), lambda bi, i: (bi, 0, Z_V)),
            pl.BlockSpec(bias_tab.shape, lambda bi, i: (0, 0, 0, 0), pipeline_mode=pl.Buffered(1)),
        ],
        out_specs=pl.BlockSpec((None, tq, ATT_W), lambda bi, i: (bi, i, 0)),
        scratch_shapes=[pltpu.VMEM((2, ATT_W, l), BF16),
                        pltpu.VMEM((2, ATT_UNROLL, N_HEADS // 2, 2 * GRID_W, WIN_ROWS * GRID_W), F32),
                        pltpu.VMEM((2, ATT_UNROLL, N_HEADS // 2, 2 * GRID_W, 128), F32)],
        compiler_params=pltpu.CompilerParams(dimension_semantics=("arbitrary", "arbitrary")),
        name="nbr_attention",
    )(z3, z3, z3, bias_tab)


def _outproj_kernel(winv_ref, bp_ref, x_ref, u_ref, x0_ref, gh_ref, at_ref, ga_ref, hb_ref, onh_ref, ona_ref,
                    w_ref, o_ref, wk_scr):
    @pl.when((pl.program_id(0) == 0) & (pl.program_id(1) == 0))
    def _():
        _expand_kron(winv_ref, wk_scr)

    n1, rows, _ = x_ref.shape
    k = bp_ref.shape[0]
    tok = n1 * SUBLANES
    load = lambda ref: ref[...].astype(F32)
    bp, u, x0, gh, at, ga = (load(r) for r in (bp_ref, u_ref, x0_ref, gh_ref, at_ref, ga_ref))
    for h in range(rows // SUBLANES):
        sl = slice(h * SUBLANES, (h + 1) * SUBLANES)
        part = lambda v: v[:, sl, :].reshape(tok, v.shape[-1])
        ylong = _dot(wk_scr[...], bp[:, sl, :].reshape(k * SUBLANES, HY_W).astype(BF16))
        yh = part(x0) * (ylong + part(u) * hb_ref[...])
        yh = yh * lax.rsqrt(jnp.mean(yh * yh, axis=-1, keepdims=True) + EPS) * onh_ref[...]
        yh = yh * jax.nn.silu(part(gh))
        ya = part(at)
        ya = ya * lax.rsqrt(jnp.mean(ya * ya, axis=-1, keepdims=True) + EPS) * ona_ref[...]
        ya = ya * jax.nn.silu(part(ga))
        acc = _dot(yh.astype(BF16), w_ref[:HY_W, :]) + _dot(ya.astype(BF16), w_ref[HY_W:, :])
        o_ref[:, sl, :] = (x_ref[:, sl, :].reshape(tok, D_MODEL) + acc).reshape(n1, SUBLANES, D_MODEL)


def _outproj(w_inv, bp, x, u, x0c, z, att, hy_bias, on_hy, on_att, w_out_bf):
    p, n1, n2, _ = x.shape
    rows = SLAB_ROWS * max(1, OUT_SLABS // n1)
    blk = lambda width, c: pl.BlockSpec((None, n1, rows, width), lambda pi, j, c=c: (pi, 0, j, c))
    vec = pl.BlockSpec((1, CB), lambda pi, j: (0, 0))
    return pl.pallas_call(
        _outproj_kernel,
        out_shape=jax.ShapeDtypeStruct(x.shape, F32),
        grid=(p, n2 // rows),
        in_specs=[pl.BlockSpec(w_inv.shape, lambda pi, j: (0, 0)),
                  pl.BlockSpec((None, 2 * n1, rows, CB), lambda pi, j: (pi, 0, j, 0)),
                  blk(D_MODEL, 0), blk(CB, 0), blk(CB, 0), blk(CB, Z_GATE_HY), blk(CB, 0), blk(CB, Z_GATE_ATT),
                  vec, vec, vec, pl.BlockSpec((HY_W + ATT_W, D_MODEL), lambda pi, j: (0, 0))],
        out_specs=blk(D_MODEL, 0),
        scratch_shapes=[pltpu.VMEM((n1 * SUBLANES, 2 * n1 * SUBLANES), BF16)],
        compiler_params=pltpu.CompilerParams(dimension_semantics=("arbitrary", "arbitrary")),
        name="outproj",
    )(w_inv, bp, x, u, x0c, z, att, z, hy_bias, on_hy, on_att, w_out_bf)


def _dft_tables(l):
    n = 2 * l
    n2 = DFT_N2
    n1 = n // n2
    h = n1 // 2
    k1 = np.arange(n1)
    ang1 = -2.0 * np.pi * ((k1[:, None] * k1[None, :]) % n1) / n1
    f1r, f1i = np.cos(ang1), np.sin(ang1)
    w_data = np.block([[f1r[:, :h], -f1i[:, :h]], [f1i[:, :h], f1r[:, :h]]])
    w_filt = np.concatenate([f1r, f1i], axis=0)
    w_inv = np.block([[f1r[:h], f1i[:h]], [-f1i[:h], f1r[:h]]]) / n
    k2 = np.arange(n2)
    ang2 = -2.0 * np.pi * ((k2[:, None] * k2[None, :]) % n2) / n2
    angt = -2.0 * np.pi * (k1[:, None] * k2[None, :]) / n
    f = lambda a: jnp.asarray(a, dtype=F32)
    return dict(n1=n1, w_data=f(w_data), w_filt=f(w_filt), w_inv=f(w_inv),
                f2r=f(np.cos(ang2)), f2i=f(np.sin(ang2)), twr=f(np.cos(angt)), twi=f(np.sin(angt)))


def _positional_features(l):
    t01 = jnp.linspace(0.0, 1.0, l, dtype=F32)[None, :]
    w = 2.0 * math.pi * jnp.arange(l, dtype=F32)[None, :] / l
    f = jnp.linspace(1e-4, POS_BANDS - 1, POS_BANDS, dtype=F32)[:, None]
    zt = jnp.concatenate([t01, jnp.cos(f * w), -jnp.sin(f * w)], axis=0)
    return jnp.pad(zt, ((0, 128 - zt.shape[0]), (0, 0))), t01.reshape(l, 1)


def _abs_deltas():
    min_decay = math.log(1e-2) / 1.5
    max_decay = math.log(1e-2) / 0.3
    return jnp.abs(jnp.linspace(min_decay, max_decay, 2 * HY_W, dtype=F32))[None, :]


def _bias_table(rpb):
    cols = np.arange(GRID_W)
    cstart = np.clip(cols - WIN_COLS // 2, 0, GRID_W - WIN_COLS)
    kc = np.arange(GRID_W)
    valid = (kc[None, :] >= cstart[:, None]) & (kc[None, :] < cstart[:, None] + WIN_COLS)
    rel = kc[None, :] - cols[:, None] + (WIN_COLS - 1)
    n_rel = 2 * WIN_COLS - 1
    ci, ki = np.nonzero(valid)
    sel = np.zeros((2, n_rel, GRID_W, 2, GRID_W), np.float32)
    for r in range(2):
        sel[r, rel[ci, ki], ci, r, ki] = 1.0
    outside = np.broadcast_to(np.where(valid, 0.0, NEG_INF)[:, None, :], (GRID_W, 2, GRID_W)).astype(np.float32)
    n_off = 2 * WIN_ROWS - 2
    pairs = jnp.stack([rpb[:, :-1], rpb[:, 1:]], axis=2).astype(F32)
    lhs = pairs.transpose(1, 0, 2, 3).reshape(n_off * N_HEADS, 2 * n_rel) * LOG2E
    t = jnp.dot(lhs, jnp.asarray(sel.reshape(2 * n_rel, -1)), precision=lax.Precision.HIGHEST)
    t = t + jnp.asarray(outside.reshape(1, -1))
    return t.reshape(n_off, N_HEADS // 2, 2 * GRID_W, 2 * GRID_W)


def _filter_spectrum(tabs, zpos, absd, f_w1, f_b1, f_fr1, f_w2, f_b2, f_fr2, f_w3):
    n1 = tabs["n1"]
    w1p = jnp.pad(f_w1, ((0, 128 - f_w1.shape[0]), (0, 0)))
    halves = _filter(zpos, w1p, f_b1[None], f_fr1[None], f_w2, f_b2[None], f_fr2[None], f_w3, absd)
    a = _slab_stage(tabs["w_filt"], [h.reshape(1, n1 // 2, DFT_N2, HY_W) for h in halves], BF16)
    return _mid_filter(a.reshape(2, n1, DFT_N2, HY_W), tabs["f2r"], tabs["f2i"], tabs["twr"], tabs["twi"])


def _layer(x, tabs, kf, bias_tab, bd, norm_g, w_in, conv_w, conv_b, hy_bias, qn_g, kn_g, on_hy, on_att, w_out):
    b, l, _ = x.shape
    t = b * l
    n1 = tabs["n1"]
    x2 = x.reshape(t, D_MODEL)
    qgain = jnp.tile(qn_g, N_HEADS)[None] * (HEAD_DIM ** -0.5 * LOG2E)
    kgain = jnp.tile(kn_g, N_HEADS)[None]
    z2, u, x0c = _inproj(x2, l, norm_g[None], w_in.astype(BF16), qgain, kgain, bd, conv_w, conv_b[None])
    z3 = z2.reshape(b, l, z2.shape[1])
    a = _slab_stage(tabs["w_data"], [u.reshape(b // 2, n1, DFT_N2, HY_W)], BF16)
    bp = _mid_data(a.reshape(b // 2, 2, n1, DFT_N2, HY_W), kf, tabs["f2r"], tabs["f2i"], tabs["twr"], tabs["twi"])
    att = _attention(z3, bias_tab)
    view = lambda v: v.reshape(b // 2, n1, DFT_N2, v.shape[-1])
    out = _outproj(tabs["w_inv"], bp.reshape(b // 2, 2 * n1, DFT_N2, HY_W), view(x), view(u), view(x0c), view(z2),
                   view(att), hy_bias[None], on_hy[None], on_att[None], w_out.astype(BF16))
    return out.reshape(b, l, D_MODEL)


def _trunk(x, norm_g, w_in, conv_w, conv_b, f_w1, f_b1, f_fr1, f_w2, f_b2, f_fr2, f_w3,
           hy_bias, qn_g, kn_g, rpb, on_hy, on_att, w_out):
    b, l, _ = x.shape
    assert b % 2 == 0 and l % (GRID_W * ATT_ROWS) == 0 and l // GRID_W >= WIN_ROWS
    assert (2 * l) % (DFT_N2 * MID_KB) == 0 and l % TOKEN_TILE == 0
    tabs = _dft_tables(l)
    zpos = _positional_features(l)
    absd = _abs_deltas()
    head = np.arange(CB) // HEAD_DIM
    bd = jnp.asarray((head[:, None] == head[None, :]).astype(np.float32) / HEAD_DIM, dtype=BF16)
    for i in range(norm_g.shape[0]):
        kf = _filter_spectrum(tabs, zpos, absd, f_w1[i], f_b1[i], f_fr1[i], f_w2[i], f_b2[i], f_fr2[i], f_w3[i])
        x = _layer(x, tabs, kf, _bias_table(rpb[i]), bd, norm_g[i], w_in[i], conv_w[i], conv_b[i], hy_bias[i],
                   qn_g[i], kn_g[i], on_hy[i], on_att[i], w_out[i])
    return x


def kernel(x_prompt, x_sample, norm_g, w_in, conv_w, conv_b, f_w1, f_b1, f_fr1, f_w2, f_b2, f_fr2, f_w3,
           hy_bias, qn_g, kn_g, rpb, on_hy, on_att, w_out):
    params = (norm_g, w_in, conv_w, conv_b, f_w1, f_b1, f_fr1, f_w2, f_b2, f_fr2, f_w3,
              hy_bias, qn_g, kn_g, rpb, on_hy, on_att, w_out)
    return (_trunk(x_prompt, *params), _trunk(x_sample, *params))
```

```python
import functools
import math

import numpy as np
import jax
import jax.numpy as jnp
from jax import lax
from jax.experimental import pallas as pl
from jax.experimental.pallas import tpu as pltpu

F32 = jnp.float32
BF16 = jnp.bfloat16

D_MODEL = 1024
GRID_W = 64
HY_W = 512
ATT_W = 512
HEAD_DIM = 64
N_HEADS = 8
WIN_ROWS = 8
WIN_COLS = 16
POS_BANDS = 16
FILTER_HIDDEN = 64
N_IN = 4 * HY_W + 4 * ATT_W
EPS = 1e-6
NEG_INF = -1e30

CB = 512
DFT_N2 = 256
TOKEN_TILE = 1024
N_HY_SLOTS = 3
Z_GATE_HY, Z_Q, Z_K, Z_V, Z_GATE_ATT = range(5)
ATT_ROWS = 8
KT_UNIT = 128
ATT_UNROLL = 2
LOG2E = math.log2(math.e)
MID_KB = 8
MID_COLS = 256
SUBLANES = 8
SLAB_ROWS = 16
SLAB_TARGET = 512
OUT_SLABS = 64


def _dot(a, b):
    return jnp.dot(a, b, preferred_element_type=F32)


def _dot_hi(a, b):
    return jnp.dot(a, b, preferred_element_type=F32, precision=lax.Precision.HIGHEST)


def _split(x):
    hi = x.astype(BF16)
    lo = (x - hi.astype(F32)).astype(BF16)
    return hi, lo


def _dot3(a, b):
    ah, al = _split(a)
    bh, bl = _split(b)
    return _dot(ah, bh) + (_dot(ah, bl) + _dot(al, bh))


def _inproj_kernel(x_ref, xp_ref, xn_ref, g_ref, w_ref, qg_ref, kg_ref, bd_ref, cw_ref, cb_ref,
                   z_ref, u_ref, x0_ref, *, tiles_per_seq):
    pos = pl.program_id(0) % tiles_per_seq
    tm = x_ref.shape[0]

    def norm(x):
        ms = jnp.mean(x * x, axis=-1, keepdims=True)
        return (x * lax.rsqrt(ms + EPS) * g_ref[...]).astype(BF16)

    h = norm(x_ref[...])
    hp = norm(xp_ref[...])
    hn = norm(xn_ref[...])
    rows = lax.broadcasted_iota(jnp.int32, (tm, CB), 0)
    conv = []
    for j in range(N_HY_SLOTS):
        cols = slice(j * CB, (j + 1) * CB)
        cur = _dot(h, w_ref[:, cols])
        prev_row = jnp.where(pos > 0, _dot(hp, w_ref[:, cols])[SUBLANES - 1:SUBLANES, :], 0.0)
        next_row = jnp.where(pos < tiles_per_seq - 1, _dot(hn, w_ref[:, cols])[0:1, :], 0.0)
        dn = jnp.where(rows == 0, prev_row, pltpu.roll(cur, 1, 0))
        up = jnp.where(rows == tm - 1, next_row, pltpu.roll(cur, tm - 1, 0))
        cw = cw_ref[:, cols]
        conv.append(cw[0:1, :] * dn + cw[1:2, :] * cur + cw[2:3, :] * up + cb_ref[:, cols])
    x0_ref[...] = conv[0].astype(x0_ref.dtype)
    u_ref[...] = (conv[1] * conv[2]).astype(u_ref.dtype)
    for j in range(N_HY_SLOTS, N_IN // CB):
        zj = _dot(h, w_ref[:, j * CB:(j + 1) * CB])
        if j in (4, 5):
            hm = _dot((zj * zj).astype(BF16), bd_ref[...])
            gain = qg_ref[...] if j == 4 else kg_ref[...]
            zj = zj * lax.rsqrt(hm + EPS) * gain
        z_ref[:, (j - N_HY_SLOTS) * CB:(j - N_HY_SLOTS + 1) * CB] = zj.astype(z_ref.dtype)


def _inproj(x2, seq_len, norm_g, w_in_bf, qgain, kgain, bd, conv_w, conv_b):
    t = x2.shape[0]
    tm = TOKEN_TILE
    hb = tm // SUBLANES
    nh = t // SUBLANES
    const = lambda i: (0, 0)
    row = lambda w: pl.BlockSpec((tm, w), lambda i: (i, 0))
    z_w = N_IN - N_HY_SLOTS * CB
    return pl.pallas_call(
        functools.partial(_inproj_kernel, tiles_per_seq=seq_len // tm),
        out_shape=(jax.ShapeDtypeStruct((t, z_w), BF16), jax.ShapeDtypeStruct((t, CB), BF16),
                   jax.ShapeDtypeStruct((t, CB), BF16)),
        grid=(t // tm,),
        in_specs=[
            row(D_MODEL),
            pl.BlockSpec((SUBLANES, D_MODEL), lambda i: (jnp.maximum(i * hb - 1, 0), 0)),
            pl.BlockSpec((SUBLANES, D_MODEL), lambda i: (jnp.minimum((i + 1) * hb, nh - 1), 0)),
            pl.BlockSpec((1, D_MODEL), const),
            pl.BlockSpec((D_MODEL, N_IN), const),
            pl.BlockSpec((1, CB), const),
            pl.BlockSpec((1, CB), const),
            pl.BlockSpec((CB, CB), const),
            pl.BlockSpec((3, N_HY_SLOTS * CB), const),
            pl.BlockSpec((1, N_HY_SLOTS * CB), const),
        ],
        out_specs=(row(z_w), row(CB), row(CB)),
        compiler_params=pltpu.CompilerParams(dimension_semantics=("arbitrary",)),
        name="inproj",
    )(x2, x2, x2, norm_g, w_in_bf, qgain, kgain, bd, conv_w, conv_b)


def _filter_kernel(zt_ref, t_ref, w1t_ref, b1_ref, fr1_ref, w2t_ref, b2_ref, fr2_ref, w3_ref, dl_ref,
                   fwd_ref, bwd_ref, carry_ref):
    tn = fwd_ref.shape[0]

    @pl.when(pl.program_id(0) == 0)
    def _():
        carry_ref[...] = jnp.zeros_like(carry_ref)

    h = jnp.sin(fr1_ref[...] * (_dot_hi(w1t_ref[...], zt_ref[...]) + b1_ref[...]))
    h = jnp.sin(fr2_ref[...] * (_dot_hi(w2t_ref[...], h) + b2_ref[...]))
    h = _dot3(h.T, w3_ref[...])
    h = h * jnp.exp(-t_ref[...] * dl_ref[...])
    fwd_ref[...] = h[:, :HY_W].astype(fwd_ref.dtype)
    hb = h[:, HY_W:].astype(BF16)
    i_out = lax.broadcasted_iota(jnp.int32, (tn, tn), 0)
    i_in = lax.broadcasted_iota(jnp.int32, (tn, tn), 1)
    flip = jnp.where((i_out + i_in == tn) & (i_out > 0), 1.0, 0.0).astype(BF16)
    rev = _dot(flip, hb)
    first_row = lax.broadcasted_iota(jnp.int32, rev.shape, 0) == 0
    bwd_ref[...] = jnp.where(first_row, carry_ref[0:1, :], rev).astype(bwd_ref.dtype)
    carry_ref[...] = hb[:carry_ref.shape[0], :].astype(F32)


def _filter(zpos, w1p, b1, fr1, w2, b2, fr2, w3, absdeltas):
    zpos_t, tcol = zpos
    l = tcol.shape[0]
    tn = min(512, l)
    nb = l // tn
    const = lambda i: (0, 0)
    col = pl.BlockSpec((FILTER_HIDDEN, 1), const)
    half = jax.ShapeDtypeStruct((l, HY_W), BF16)
    return pl.pallas_call(
        _filter_kernel,
        out_shape=(half, half),
        grid=(nb,),
        in_specs=[
            pl.BlockSpec((128, tn), lambda i: (0, nb - 1 - i)),
            pl.BlockSpec((tn, 1), lambda i: (nb - 1 - i, 0)),
            pl.BlockSpec((FILTER_HIDDEN, 128), const),
            col, col,
            pl.BlockSpec((FILTER_HIDDEN, FILTER_HIDDEN), const),
            col, col,
            pl.BlockSpec((FILTER_HIDDEN, 2 * HY_W), const),
            pl.BlockSpec((1, 2 * HY_W), const),
        ],
        out_specs=(pl.BlockSpec((tn, HY_W), lambda i: (nb - 1 - i, 0)), pl.BlockSpec((tn, HY_W), lambda i: (i, 0))),
        scratch_shapes=[pltpu.VMEM((SUBLANES, HY_W), F32)],
        compiler_params=pltpu.CompilerParams(dimension_semantics=("arbitrary",)),
        name="hyena_filter",
    )(zpos_t, tcol, w1p.T, b1.T, fr1.T, w2.T, b2.T, fr2.T, w3, absdeltas)


def _expand_kron(w_ref, wk_scr):
    m, k = w_ref.shape
    m8, k8 = wk_scr.shape
    onehot = lambda cond: jnp.where(cond, 1.0, 0.0).astype(BF16)
    col_exp = onehot(lax.broadcasted_iota(jnp.int32, (k, k8), 1) // SUBLANES
                     == lax.broadcasted_iota(jnp.int32, (k, k8), 0))
    wcols = _dot(w_ref[...].astype(BF16), col_exp).astype(BF16)
    rb = min(m8, 256)
    for r in range(m8 // rb):
        rows = r * rb + lax.broadcasted_iota(jnp.int32, (rb, m), 0)
        row_exp = onehot(rows // SUBLANES == lax.broadcasted_iota(jnp.int32, (rb, m), 1))
        blk = _dot(row_exp, wcols)
        same = (lax.broadcasted_iota(jnp.int32, (rb, k8), 0) % SUBLANES
                == lax.broadcasted_iota(jnp.int32, (rb, k8), 1) % SUBLANES)
        wk_scr[r * rb:(r + 1) * rb, :] = jnp.where(same, blk, 0.0).astype(BF16)


def _slab_kernel(w_ref, *refs):
    x_refs, o_ref, wk_scr = refs[:-2], refs[-2], refs[-1]
    k = sum(r.shape[0] for r in x_refs)
    _, sub, c = x_refs[0].shape
    m = o_ref.shape[0]

    @pl.when((pl.program_id(0) == 0) & (pl.program_id(1) == 0))
    def _():
        _expand_kron(w_ref, wk_scr)

    x = jnp.concatenate([r[...].astype(F32) for r in x_refs], axis=0)
    halves = []
    for h in range(sub // SUBLANES):
        xh = x[:, h * SUBLANES:(h + 1) * SUBLANES, :].reshape(k * SUBLANES, c)
        halves.append(_dot(wk_scr[...], xh.astype(BF16)).reshape(m, SUBLANES, c))
    o_ref[...] = jnp.concatenate(halves, axis=1).astype(o_ref.dtype)


def _slab_stage(w, xs, out_dtype):
    p, _, n2, c = xs[0].shape
    k = sum(x.shape[1] for x in xs)
    m = w.shape[0]
    assert w.shape == (m, k)
    rows = min(n2, SLAB_ROWS * max(1, SLAB_TARGET // max(m, k)))
    return pl.pallas_call(
        _slab_kernel,
        out_shape=jax.ShapeDtypeStruct((p, m, n2, c), out_dtype),
        grid=(p, n2 // rows),
        in_specs=[pl.BlockSpec((m, k), lambda pi, j: (0, 0))]
        + [pl.BlockSpec((None, x.shape[1], rows, c), lambda pi, j: (pi, 0, j, 0)) for x in xs],
        out_specs=pl.BlockSpec((None, m, rows, c), lambda pi, j: (pi, 0, j, 0)),
        scratch_shapes=[pltpu.VMEM((m * SUBLANES, k * SUBLANES), BF16)],
        compiler_params=pltpu.CompilerParams(dimension_semantics=("arbitrary", "arbitrary")),
        name="dft_outer",
    )(w, *xs)


def _stacked_g(f2r, f2i, tr, ti):
    gr = f2r * tr - f2i * ti
    gi = f2r * ti + f2i * tr
    return gr, gi


def _stack(a, b, c, d):
    return jnp.concatenate([jnp.concatenate([a, b], axis=1), jnp.concatenate([c, d], axis=1)], axis=0)


def _mid_data_kernel(a_ref, kf_ref, f2r_ref, f2i_ref, twr_ref, twi_ref, o_ref, gs_scr, gh_scr):
    @pl.when(pl.program_id(1) == 0)
    def _():
        f2r = f2r_ref[...]
        f2i = f2i_ref[...]
        for j in range(MID_KB):
            gr, gi = _stacked_g(f2r, f2i, twr_ref[j:j + 1, :], twi_ref[j:j + 1, :])
            gs_scr[j] = _stack(gr, -gi, gi, gr).astype(BF16)
            grt, git = gr.T, gi.T
            gh_scr[j] = _stack(grt, git, -git, grt).astype(BF16)

    c = a_ref.shape[-1]
    for j in range(MID_KB):
        for cb in range(c // MID_COLS):
            cols = slice(cb * MID_COLS, (cb + 1) * MID_COLS)
            a = jnp.concatenate([a_ref[0, j, :, cols], a_ref[1, j, :, cols]], axis=0)
            x = _dot(gs_scr[j], a)
            xr, xi = x[:DFT_N2], x[DFT_N2:]
            kr, ki = kf_ref[0, j, :, cols].astype(F32), kf_ref[1, j, :, cols].astype(F32)
            y = jnp.concatenate([xr * kr - xi * ki, xr * ki + xi * kr], axis=0).astype(BF16)
            bp = _dot(gh_scr[j], y)
            o_ref[0, j, :, cols] = bp[:DFT_N2].astype(o_ref.dtype)
            o_ref[1, j, :, cols] = bp[DFT_N2:].astype(o_ref.dtype)


def _mid_filter_kernel(a_ref, f2r_ref, f2i_ref, twr_ref, twi_ref, o_ref):
    f2r = f2r_ref[...]
    f2i = f2i_ref[...]
    for j in range(MID_KB):
        gr, gi = _stacked_g(f2r, f2i, twr_ref[j:j + 1, :], twi_ref[j:j + 1, :])
        gs = _stack(gr, -gi, gi, gr).astype(BF16)
        a = jnp.concatenate([a_ref[0, j], a_ref[1, j]], axis=0)
        x = _dot(gs, a)
        o_ref[0, j] = x[:DFT_N2].astype(o_ref.dtype)
        o_ref[1, j] = x[DFT_N2:].astype(o_ref.dtype)


def _mid_specs(n1):
    sq = pl.BlockSpec((DFT_N2, DFT_N2), lambda k, p: (0, 0))
    tw = pl.BlockSpec((MID_KB, DFT_N2), lambda k, p: (k, 0))
    return sq, tw


def _mid_data(a5, kf, f2r, f2i, twr, twi):
    p, _, n1, n2, c = a5.shape
    sq, tw = _mid_specs(n1)
    blk = pl.BlockSpec((None, 2, MID_KB, n2, c), lambda k, pi: (pi, 0, k, 0, 0))
    return pl.pallas_call(
        _mid_data_kernel,
        out_shape=jax.ShapeDtypeStruct(a5.shape, BF16),
        grid=(n1 // MID_KB, p),
        in_specs=[blk, pl.BlockSpec((2, MID_KB, n2, c), lambda k, pi: (0, k, 0, 0)), sq, sq, tw, tw],
        out_specs=blk,
        scratch_shapes=[pltpu.VMEM((MID_KB, 2 * n2, 2 * n2), BF16), pltpu.VMEM((MID_KB, 2 * n2, 2 * n2), BF16)],
        compiler_params=pltpu.CompilerParams(dimension_semantics=("arbitrary", "arbitrary")),
        name="dft_mid",
    )(a5, kf, f2r, f2i, twr, twi)


def _mid_filter(a4, f2r, f2i, twr, twi):
    _, n1, n2, c = a4.shape
    sq, tw = _mid_specs(n1)
    blk = pl.BlockSpec((2, MID_KB, n2, c), lambda k, pi: (0, k, 0, 0))
    return pl.pallas_call(
        _mid_filter_kernel,
        out_shape=jax.ShapeDtypeStruct(a4.shape, BF16),
        grid=(n1 // MID_KB, 1),
        in_specs=[blk, sq, sq, tw, tw],
        out_specs=blk,
        compiler_params=pltpu.CompilerParams(dimension_semantics=("arbitrary", "arbitrary")),
        name="dft_mid_filter",
    )(a4, f2r, f2i, twr, twi)


def _attn_kernel(q_ref, k_ref, v_ref, bias_ref, o_ref, kt_scr, s_scr, m_scr, *, rows):
    rb = pl.program_id(1)
    lane = lax.broadcasted_iota(jnp.int32, (GRID_W, 128), 1)
    first = lane < HEAD_DIM
    band = WIN_ROWS * GRID_W
    ones = jnp.ones((band, 128), BF16)
    n_keys = rows * GRID_W
    units_per_block = ATT_ROWS * GRID_W // KT_UNIT
    n_trips = ATT_ROWS // ATT_UNROLL
    assert units_per_block == n_trips
    pairs = range(N_HEADS // 2)

    def transpose_unit(m):
        lanes = pl.ds(pl.multiple_of(m * KT_UNIT, KT_UNIT), KT_UNIT)
        kt_scr[0, :, lanes] = k_ref[pl.ds(pl.multiple_of(m * KT_UNIT, KT_UNIT), KT_UNIT), :].T
        src = jnp.minimum(m * KT_UNIT + GRID_W, n_keys - KT_UNIT)
        kt_scr[1, :, lanes] = k_ref[pl.ds(pl.multiple_of(src, GRID_W), KT_UNIT), :].T

    @pl.when(rb == 0)
    def _():
        for j in range(units_per_block):
            transpose_unit(j)

    nxt = jnp.minimum(rb + 1, pl.num_programs(1) - 1)

    def row_info(t):
        info = []
        for a in range(ATT_UNROLL):
            rl = t * ATT_UNROLL + a
            r = rb * ATT_ROWS + rl
            r0 = jnp.clip(r - WIN_ROWS // 2, 0, rows - WIN_ROWS)
            par = r0 % 2
            info.append((r - r0, par, (r0 - par) * GRID_W, r0 * GRID_W, pl.multiple_of(rl * GRID_W, GRID_W)))
        return info

    def scores(t, slot):
        for a, (d, par, kt_start, start, qoff) in enumerate(row_info(t)):
            for pr in pairs:
                cs = slice(pr * 128, (pr + 1) * 128)
                q2 = q_ref[pl.ds(qoff, GRID_W), cs]
                zero = jnp.zeros_like(q2)
                qbd = jnp.concatenate([jnp.where(first, q2, zero), jnp.where(first, zero, q2)], axis=0)
                bias = [bias_ref[2 * j + (WIN_ROWS - 1) - d, pr] for j in range(band // KT_UNIT)]
                kl = pl.ds(pl.multiple_of(kt_start, KT_UNIT), band)
                s = _dot(qbd, kt_scr[par, cs, kl]) + jnp.concatenate(bias, axis=1)
                s_scr[slot, a, pr] = s
                m_scr[slot, a, pr] = jnp.broadcast_to(jnp.max(s, axis=-1, keepdims=True), (2 * GRID_W, 128))

    def outputs(t, slot):
        for a, (d, par, kt_start, start, qoff) in enumerate(row_info(t)):
            for pr in pairs:
                cs = slice(pr * 128, (pr + 1) * 128)
                m = m_scr[slot, a, pr]
                p = jnp.exp2(s_scr[slot, a, pr] - jnp.concatenate([m] * (band // 128), axis=1)).astype(BF16)
                ks = pl.ds(pl.multiple_of(start, GRID_W), band)
                acc = _dot(p, jnp.concatenate([v_ref[ks, cs], ones], axis=1))
                o = acc[:, :128] / acc[:, 128:]
                o_ref[pl.ds(qoff, GRID_W), cs] = jnp.where(first, o[:GRID_W], o[GRID_W:]).astype(o_ref.dtype)

    scores(0, 0)
    for t in range(n_trips):
        transpose_unit(nxt * units_per_block + t)
        if t + 1 < n_trips:
            scores(t + 1, (t + 1) % 2)
        outputs(t, t % 2)


def _attention(z3, bias_tab):
    b, l, _ = z3.shape
    rows = l // GRID_W
    tq = ATT_ROWS * GRID_W
    return pl.pallas_call(
        functools.partial(_attn_kernel, rows=rows),
        out_shape=jax.ShapeDtypeStruct((b, l, ATT_W), BF16),
        grid=(b, rows // ATT_ROWS),
        in_specs=[
            pl.BlockSpec((None, tq, CB), lambda bi, i: (bi, i, Z_Q)),
            pl.BlockSpec((None, l, CB), lambda bi, i: (bi, 0, Z_K), pipeline_mode=pl.Buffered(1)),
            pl.BlockSpec((None, l, CB), lambda bi, i: (bi, 0, Z_V)),
            pl.BlockSpec(bias_tab.shape, lambda bi, i: (0, 0, 0, 0), pipeline_mode=pl.Buffered(1)),
        ],
        out_specs=pl.BlockSpec((None, tq, ATT_W), lambda bi, i: (bi, i, 0)),
        scratch_shapes=[pltpu.VMEM((2, ATT_W, l), BF16),
                        pltpu.VMEM((2, ATT_UNROLL, N_HEADS // 2, 2 * GRID_W, WIN_ROWS * GRID_W), F32),
                        pltpu.VMEM((2, ATT_UNROLL, N_HEADS // 2, 2 * GRID_W, 128), F32)],
        compiler_params=pltpu.CompilerParams(dimension_semantics=("arbitrary", "arbitrary")),
        name="nbr_attention",
    )(z3, z3, z3, bias_tab)


def _outproj_kernel(winv_ref, bp_ref, x_ref, u_ref, x0_ref, gh_ref, at_ref, ga_ref, hb_ref, onh_ref, ona_ref,
                    w_ref, o_ref, wk_scr):
    @pl.when((pl.program_id(0) == 0) & (pl.program_id(1) == 0))
    def _():
        _expand_kron(winv_ref, wk_scr)

    n1, rows, _ = x_ref.shape
    k = bp_ref.shape[0]
    tok = n1 * SUBLANES
    load = lambda ref: ref[...].astype(F32)
    bp, u, x0, gh, at, ga = (load(r) for r in (bp_ref, u_ref, x0_ref, gh_ref, at_ref, ga_ref))
    for h in range(rows // SUBLANES):
        sl = slice(h * SUBLANES, (h + 1) * SUBLANES)
        part = lambda v: v[:, sl, :].reshape(tok, v.shape[-1])
        ylong = _dot(wk_scr[...], bp[:, sl, :].reshape(k * SUBLANES, HY_W).astype(BF16))
        yh = part(x0) * (ylong + part(u) * hb_ref[...])
        yh = yh * lax.rsqrt(jnp.mean(yh * yh, axis=-1, keepdims=True) + EPS) * onh_ref[...]
        yh = yh * jax.nn.silu(part(gh))
        ya = part(at)
        ya = ya * lax.rsqrt(jnp.mean(ya * ya, axis=-1, keepdims=True) + EPS) * ona_ref[...]
        ya = ya * jax.nn.silu(part(ga))
        acc = _dot(yh.astype(BF16), w_ref[:HY_W, :]) + _dot(ya.astype(BF16), w_ref[HY_W:, :])
        o_ref[:, sl, :] = (x_ref[:, sl, :].reshape(tok, D_MODEL) + acc).reshape(n1, SUBLANES, D_MODEL)


def _outproj(w_inv, bp, x, u, x0c, z, att, hy_bias, on_hy, on_att, w_out_bf):
    p, n1, n2, _ = x.shape
    rows = min(n2, SLAB_ROWS * max(1, OUT_SLABS // n1))
    blk = lambda width, c: pl.BlockSpec((None, n1, rows, width), lambda pi, j, c=c: (pi, 0, j, c))
    vec = pl.BlockSpec((1, CB), lambda pi, j: (0, 0))
    return pl.pallas_call(
        _outproj_kernel,
        out_shape=jax.ShapeDtypeStruct(x.shape, F32),
        grid=(p, n2 // rows),
        in_specs=[pl.BlockSpec(w_inv.shape, lambda pi, j: (0, 0)),
                  pl.BlockSpec((None, 2 * n1, rows, CB), lambda pi, j: (pi, 0, j, 0)),
                  blk(D_MODEL, 0), blk(CB, 0), blk(CB, 0), blk(CB, Z_GATE_HY), blk(CB, 0), blk(CB, Z_GATE_ATT),
                  vec, vec, vec, pl.BlockSpec((HY_W + ATT_W, D_MODEL), lambda pi, j: (0, 0))],
        out_specs=blk(D_MODEL, 0),
        scratch_shapes=[pltpu.VMEM((n1 * SUBLANES, 2 * n1 * SUBLANES), BF16)],
        compiler_params=pltpu.CompilerParams(dimension_semantics=("arbitrary", "arbitrary")),
        name="outproj",
    )(w_inv, bp, x, u, x0c, z, att, z, hy_bias, on_hy, on_att, w_out_bf)


def _dft_tables(l):
    n = 2 * l
    n2 = DFT_N2
    n1 = n // n2
    h = n1 // 2
    k1 = np.arange(n1)
    ang1 = -2.0 * np.pi * ((k1[:, None] * k1[None, :]) % n1) / n1
    f1r, f1i = np.cos(ang1), np.sin(ang1)
    w_data = np.block([[f1r[:, :h], -f1i[:, :h]], [f1i[:, :h], f1r[:, :h]]])
    w_filt = np.concatenate([f1r, f1i], axis=0)
    w_inv = np.block([[f1r[:h], f1i[:h]], [-f1i[:h], f1r[:h]]]) / n
    k2 = np.arange(n2)
    ang2 = -2.0 * np.pi * ((k2[:, None] * k2[None, :]) % n2) / n2
    angt = -2.0 * np.pi * (k1[:, None] * k2[None, :]) / n
    f = lambda a: jnp.asarray(a, dtype=F32)
    return dict(n1=n1, w_data=f(w_data), w_filt=f(w_filt), w_inv=f(w_inv),
                f2r=f(np.cos(ang2)), f2i=f(np.sin(ang2)), twr=f(np.cos(angt)), twi=f(np.sin(angt)))


def _positional_features(l):
    t01 = jnp.linspace(0.0, 1.0, l, dtype=F32)[None, :]
    w = 2.0 * math.pi * jnp.arange(l, dtype=F32)[None, :] / l
    f = jnp.linspace(1e-4, POS_BANDS - 1, POS_BANDS, dtype=F32)[:, None]
    zt = jnp.concatenate([t01, jnp.cos(f * w), -jnp.sin(f * w)], axis=0)
    return jnp.pad(zt, ((0, 128 - zt.shape[0]), (0, 0))), t01.reshape(l, 1)


def _abs_deltas():
    min_decay = math.log(1e-2) / 1.5
    max_decay = math.log(1e-2) / 0.3
    return jnp.abs(jnp.linspace(min_decay, max_decay, 2 * HY_W, dtype=F32))[None, :]


def _bias_table(rpb):
    cols = np.arange(GRID_W)
    cstart = np.clip(cols - WIN_COLS // 2, 0, GRID_W - WIN_COLS)
    kc = np.arange(GRID_W)
    valid = (kc[None, :] >= cstart[:, None]) & (kc[None, :] < cstart[:, None] + WIN_COLS)
    rel = kc[None, :] - cols[:, None] + (WIN_COLS - 1)
    n_rel = 2 * WIN_COLS - 1
    ci, ki = np.nonzero(valid)
    sel = np.zeros((2, n_rel, GRID_W, 2, GRID_W), np.float32)
    for r in range(2):
        sel[r, rel[ci, ki], ci, r, ki] = 1.0
    outside = np.broadcast_to(np.where(valid, 0.0, NEG_INF)[:, None, :], (GRID_W, 2, GRID_W)).astype(np.float32)
    n_off = 2 * WIN_ROWS - 2
    pairs = jnp.stack([rpb[:, :-1], rpb[:, 1:]], axis=2).astype(F32)
    lhs = pairs.transpose(1, 0, 2, 3).reshape(n_off * N_HEADS, 2 * n_rel) * LOG2E
    t = jnp.dot(lhs, jnp.asarray(sel.reshape(2 * n_rel, -1)), precision=lax.Precision.HIGHEST)
    t = t + jnp.asarray(outside.reshape(1, -1))
    return t.reshape(n_off, N_HEADS // 2, 2 * GRID_W, 2 * GRID_W)


def _filter_spectrum(tabs, zpos, absd, f_w1, f_b1, f_fr1, f_w2, f_b2, f_fr2, f_w3):
    n1 = tabs["n1"]
    w1p = jnp.pad(f_w1, ((0, 128 - f_w1.shape[0]), (0, 0)))
    halves = _filter(zpos, w1p, f_b1[None], f_fr1[None], f_w2, f_b2[None], f_fr2[None], f_w3, absd)
    a = _slab_stage(tabs["w_filt"], [h.reshape(1, n1 // 2, DFT_N2, HY_W) for h in halves], BF16)
    return _mid_filter(a.reshape(2, n1, DFT_N2, HY_W), tabs["f2r"], tabs["f2i"], tabs["twr"], tabs["twi"])


def _layer(x, tabs, kf, bias_tab, bd, norm_g, w_in, conv_w, conv_b, hy_bias, qn_g, kn_g, on_hy, on_att, w_out):
    b, l, _ = x.shape
    t = b * l
    n1 = tabs["n1"]
    x2 = x.reshape(t, D_MODEL)
    qgain = jnp.tile(qn_g, N_HEADS)[None] * (HEAD_DIM ** -0.5 * LOG2E)
    kgain = jnp.tile(kn_g, N_HEADS)[None]
    z2, u, x0c = _inproj(x2, l, norm_g[None], w_in.astype(BF16), qgain, kgain, bd, conv_w, conv_b[None])
    z3 = z2.reshape(b, l, z2.shape[1])
    a = _slab_stage(tabs["w_data"], [u.reshape(b // 2, n1, DFT_N2, HY_W)], BF16)
    bp = _mid_data(a.reshape(b // 2, 2, n1, DFT_N2, HY_W), kf, tabs["f2r"], tabs["f2i"], tabs["twr"], tabs["twi"])
    att = _attention(z3, bias_tab)
    view = lambda v: v.reshape(b // 2, n1, DFT_N2, v.shape[-1])
    out = _outproj(tabs["w_inv"], bp.reshape(b // 2, 2 * n1, DFT_N2, HY_W), view(x), view(u), view(x0c), view(z2),
                   view(att), hy_bias[None], on_hy[None], on_att[None], w_out.astype(BF16))
    return out.reshape(b, l, D_MODEL)


def _trunk(x, norm_g, w_in, conv_w, conv_b, f_w1, f_b1, f_fr1, f_w2, f_b2, f_fr2, f_w3,
           hy_bias, qn_g, kn_g, rpb, on_hy, on_att, w_out):
    b, l, _ = x.shape
    assert b % 2 == 0 and l % (GRID_W * ATT_ROWS) == 0 and l // GRID_W >= WIN_ROWS
    assert (2 * l) % (DFT_N2 * MID_KB) == 0 and l % TOKEN_TILE == 0
    tabs = _dft_tables(l)
    zpos = _positional_features(l)
    absd = _abs_deltas()
    head = np.arange(CB) // HEAD_DIM
    bd = jnp.asarray((head[:, None] == head[None, :]).astype(np.float32) / HEAD_DIM, dtype=BF16)
    for i in range(norm_g.shape[0]):
        kf = _filter_spectrum(tabs, zpos, absd, f_w1[i], f_b1[i], f_fr1[i], f_w2[i], f_b2[i], f_fr2[i], f_w3[i])
        x = _layer(x, tabs, kf, _bias_table(rpb[i]), bd, norm_g[i], w_in[i], conv_w[i], conv_b[i], hy_bias[i],
                   qn_g[i], kn_g[i], on_hy[i], on_att[i], w_out[i])
    return x


def kernel(x_prompt, x_sample, norm_g, w_in, conv_w, conv_b, f_w1, f_b1, f_fr1, f_w2, f_b2, f_fr2, f_w3,
           hy_bias, qn_g, kn_g, rpb, on_hy, on_att, w_out):
    params = (norm_g, w_in, conv_w, conv_b, f_w1, f_b1, f_fr1, f_w2, f_b2, f_fr2, f_w3,
              hy_bias, qn_g, kn_g, rpb, on_hy, on_att, w_out)
    return (_trunk(x_prompt, *params), _trunk(x_sample, *params))
```

```python
import functools
import math

import numpy as np
import jax
import jax.numpy as jnp
from jax import lax
from jax.experimental import pallas as pl
from jax.experimental.pallas import tpu as pltpu

F32 = jnp.float32
BF16 = jnp.bfloat16

D_MODEL = 1024
GRID_W = 64
HY_W = 512
ATT_W = 512
HEAD_DIM = 64
N_HEADS = 8
WIN_ROWS = 8
WIN_COLS = 16
POS_BANDS = 16
FILTER_HIDDEN = 64
N_IN = 4 * HY_W + 4 * ATT_W
EPS = 1e-6
NEG_INF = -1e30

CB = 512
DFT_N2 = 256
TOKEN_TILE = 1024
N_HY_SLOTS = 3
Z_GATE_HY, Z_Q, Z_K, Z_V, Z_GATE_ATT = range(5)
ATT_ROWS = 8
KT_UNIT = 128
ATT_UNROLL = 2
LOG2E = math.log2(math.e)
MID_KB = 8
MID_COLS = 256
SUBLANES = 8
SLAB_ROWS = 16
SLAB_TARGET = 512
OUT_SLABS = 64


def _dot(a, b):
    return jnp.dot(a, b, preferred_element_type=F32)


def _dot_hi(a, b):
    return jnp.dot(a, b, preferred_element_type=F32, precision=lax.Precision.HIGHEST)


def _split(x):
    hi = x.astype(BF16)
    lo = (x - hi.astype(F32)).astype(BF16)
    return hi, lo


def _dot3(a, b):
    ah, al = _split(a)
    bh, bl = _split(b)
    return _dot(ah, bh) + (_dot(ah, bl) + _dot(al, bh))


def _inproj_kernel(x_ref, xp_ref, xn_ref, g_ref, w_ref, qg_ref, kg_ref, bd_ref, cw_ref, cb_ref,
                   z_ref, u_ref, x0_ref, *, tiles_per_seq):
    pos = pl.program_id(0) % tiles_per_seq
    tm = x_ref.shape[0]

    def norm(x):
        ms = jnp.mean(x * x, axis=-1, keepdims=True)
        return (x * lax.rsqrt(ms + EPS) * g_ref[...]).astype(BF16)

    h = norm(jnp.concatenate([xp_ref[...], x_ref[...], xn_ref[...]], axis=0))
    rows = lax.broadcasted_iota(jnp.int32, (tm, CB), 0)
    conv = []
    for j in range(N_HY_SLOTS):
        cols = slice(j * CB, (j + 1) * CB)
        zext = _dot(h, w_ref[:, cols])
        cur = zext[SUBLANES:SUBLANES + tm]
        prev_row = jnp.where(pos > 0, zext[SUBLANES - 1:SUBLANES, :], 0.0)
        next_row = jnp.where(pos < tiles_per_seq - 1, zext[SUBLANES + tm:SUBLANES + tm + 1, :], 0.0)
        dn = jnp.where(rows == 0, prev_row, pltpu.roll(cur, 1, 0))
        up = jnp.where(rows == tm - 1, next_row, pltpu.roll(cur, tm - 1, 0))
        cw = cw_ref[:, cols]
        conv.append(cw[0:1, :] * dn + cw[1:2, :] * cur + cw[2:3, :] * up + cb_ref[:, cols])
    x0_ref[...] = conv[0].astype(x0_ref.dtype)
    u_ref[...] = (conv[1] * conv[2]).astype(u_ref.dtype)
    for j in range(N_HY_SLOTS, N_IN // CB):
        zj = _dot(h, w_ref[:, j * CB:(j + 1) * CB])[SUBLANES:SUBLANES + tm]
        if j in (4, 5):
            hm = _dot((zj * zj).astype(BF16), bd_ref[...])
            gain = qg_ref[...] if j == 4 else kg_ref[...]
            zj = zj * lax.rsqrt(hm + EPS) * gain
        z_ref[:, (j - N_HY_SLOTS) * CB:(j - N_HY_SLOTS + 1) * CB] = zj.astype(z_ref.dtype)


def _inproj(x2, seq_len, norm_g, w_in_bf, qgain, kgain, bd, conv_w, conv_b):
    t = x2.shape[0]
    tm = TOKEN_TILE
    hb = tm // SUBLANES
    nh = t // SUBLANES
    const = lambda i: (0, 0)
    row = lambda w: pl.BlockSpec((tm, w), lambda i: (i, 0))
    z_w = N_IN - N_HY_SLOTS * CB
    return pl.pallas_call(
        functools.partial(_inproj_kernel, tiles_per_seq=seq_len // tm),
        out_shape=(jax.ShapeDtypeStruct((t, z_w), BF16), jax.ShapeDtypeStruct((t, CB), BF16),
                   jax.ShapeDtypeStruct((t, CB), BF16)),
        grid=(t // tm,),
        in_specs=[
            row(D_MODEL),
            pl.BlockSpec((SUBLANES, D_MODEL), lambda i: (jnp.maximum(i * hb - 1, 0), 0)),
            pl.BlockSpec((SUBLANES, D_MODEL), lambda i: (jnp.minimum((i + 1) * hb, nh - 1), 0)),
            pl.BlockSpec((1, D_MODEL), const),
            pl.BlockSpec((D_MODEL, N_IN), const),
            pl.BlockSpec((1, CB), const),
            pl.BlockSpec((1, CB), const),
            pl.BlockSpec((CB, CB), const),
            pl.BlockSpec((3, N_HY_SLOTS * CB), const),
            pl.BlockSpec((1, N_HY_SLOTS * CB), const),
        ],
        out_specs=(row(z_w), row(CB), row(CB)),
        compiler_params=pltpu.CompilerParams(dimension_semantics=("arbitrary",)),
        name="inproj",
    )(x2, x2, x2, norm_g, w_in_bf, qgain, kgain, bd, conv_w, conv_b)


def _filter_kernel(zt_ref, t_ref, w1t_ref, b1_ref, fr1_ref, w2t_ref, b2_ref, fr2_ref, w3_ref, dl_ref,
                   fwd_ref, bwd_ref, carry_ref):
    tn = fwd_ref.shape[0]

    @pl.when(pl.program_id(0) == 0)
    def _():
        carry_ref[...] = jnp.zeros_like(carry_ref)

    h = jnp.sin(fr1_ref[...] * (_dot_hi(w1t_ref[...], zt_ref[...]) + b1_ref[...]))
    h = jnp.sin(fr2_ref[...] * (_dot_hi(w2t_ref[...], h) + b2_ref[...]))
    h = _dot3(h.T, w3_ref[...])
    h = h * jnp.exp(-t_ref[...] * dl_ref[...])
    fwd_ref[...] = h[:, :HY_W].astype(fwd_ref.dtype)
    hb = h[:, HY_W:].astype(BF16)
    i_out = lax.broadcasted_iota(jnp.int32, (tn, tn), 0)
    i_in = lax.broadcasted_iota(jnp.int32, (tn, tn), 1)
    flip = jnp.where((i_out + i_in == tn) & (i_out > 0), 1.0, 0.0).astype(BF16)
    rev = _dot(flip, hb)
    first_row = lax.broadcasted_iota(jnp.int32, rev.shape, 0) == 0
    bwd_ref[...] = jnp.where(first_row, carry_ref[0:1, :], rev).astype(bwd_ref.dtype)
    carry_ref[...] = hb[:carry_ref.shape[0], :].astype(F32)


def _filter(zpos, w1p, b1, fr1, w2, b2, fr2, w3, absdeltas):
    zpos_t, tcol = zpos
    l = tcol.shape[0]
    tn = min(512, l)
    nb = l // tn
    const = lambda i: (0, 0)
    col = pl.BlockSpec((FILTER_HIDDEN, 1), const)
    half = jax.ShapeDtypeStruct((l, HY_W), BF16)
    return pl.pallas_call(
        _filter_kernel,
        out_shape=(half, half),
        grid=(nb,),
        in_specs=[
            pl.BlockSpec((128, tn), lambda i: (0, nb - 1 - i)),
            pl.BlockSpec((tn, 1), lambda i: (nb - 1 - i, 0)),
            pl.BlockSpec((FILTER_HIDDEN, 128), const),
            col, col,
            pl.BlockSpec((FILTER_HIDDEN, FILTER_HIDDEN), const),
            col, col,
            pl.BlockSpec((FILTER_HIDDEN, 2 * HY_W), const),
            pl.BlockSpec((1, 2 * HY_W), const),
        ],
        out_specs=(pl.BlockSpec((tn, HY_W), lambda i: (nb - 1 - i, 0)), pl.BlockSpec((tn, HY_W), lambda i: (i, 0))),
        scratch_shapes=[pltpu.VMEM((SUBLANES, HY_W), F32)],
        compiler_params=pltpu.CompilerParams(dimension_semantics=("arbitrary",)),
        name="hyena_filter",
    )(zpos_t, tcol, w1p.T, b1.T, fr1.T, w2.T, b2.T, fr2.T, w3, absdeltas)


def _expand_kron(w_ref, wk_scr):
    m, k = w_ref.shape
    m8, k8 = wk_scr.shape
    onehot = lambda cond: jnp.where(cond, 1.0, 0.0).astype(BF16)
    col_exp = onehot(lax.broadcasted_iota(jnp.int32, (k, k8), 1) // SUBLANES
                     == lax.broadcasted_iota(jnp.int32, (k, k8), 0))
    wcols = _dot(w_ref[...].astype(BF16), col_exp).astype(BF16)
    rb = min(m8, 256)
    for r in range(m8 // rb):
        rows = r * rb + lax.broadcasted_iota(jnp.int32, (rb, m), 0)
        row_exp = onehot(rows // SUBLANES == lax.broadcasted_iota(jnp.int32, (rb, m), 1))
        blk = _dot(row_exp, wcols)
        same = (lax.broadcasted_iota(jnp.int32, (rb, k8), 0) % SUBLANES
                == lax.broadcasted_iota(jnp.int32, (rb, k8), 1) % SUBLANES)
        wk_scr[r * rb:(r + 1) * rb, :] = jnp.where(same, blk, 0.0).astype(BF16)


def _slab_kernel(w_ref, *refs):
    x_refs, o_ref, wk_scr = refs[:-2], refs[-2], refs[-1]
    k = sum(r.shape[0] for r in x_refs)
    _, sub, c = x_refs[0].shape
    m = o_ref.shape[0]

    @pl.when((pl.program_id(0) == 0) & (pl.program_id(1) == 0))
    def _():
        _expand_kron(w_ref, wk_scr)

    x = jnp.concatenate([r[...].astype(F32) for r in x_refs], axis=0)
    halves = []
    for h in range(sub // SUBLANES):
        xh = x[:, h * SUBLANES:(h + 1) * SUBLANES, :].reshape(k * SUBLANES, c)
        halves.append(_dot(wk_scr[...], xh.astype(BF16)).reshape(m, SUBLANES, c))
    o_ref[...] = jnp.concatenate(halves, axis=1).astype(o_ref.dtype)


def _slab_stage(w, xs, out_dtype):
    p, _, n2, c = xs[0].shape
    k = sum(x.shape[1] for x in xs)
    m = w.shape[0]
    assert w.shape == (m, k)
    rows = min(n2, SLAB_ROWS * max(1, SLAB_TARGET // max(m, k)))
    return pl.pallas_call(
        _slab_kernel,
        out_shape=jax.ShapeDtypeStruct((p, m, n2, c), out_dtype),
        grid=(p, n2 // rows),
        in_specs=[pl.BlockSpec((m, k), lambda pi, j: (0, 0))]
        + [pl.BlockSpec((None, x.shape[1], rows, c), lambda pi, j: (pi, 0, j, 0)) for x in xs],
        out_specs=pl.BlockSpec((None, m, rows, c), lambda pi, j: (pi, 0, j, 0)),
        scratch_shapes=[pltpu.VMEM((m * SUBLANES, k * SUBLANES), BF16)],
        compiler_params=pltpu.CompilerParams(dimension_semantics=("arbitrary", "arbitrary")),
        name="dft_outer",
    )(w, *xs)


def _stacked_g(f2r, f2i, tr, ti):
    gr = f2r * tr - f2i * ti
    gi = f2r * ti + f2i * tr
    return gr, gi


def _stack(a, b, c, d):
    return jnp.concatenate([jnp.concatenate([a, b], axis=1), jnp.concatenate([c, d], axis=1)], axis=0)


def _mid_data_kernel(a_ref, kf_ref, f2r_ref, f2i_ref, twr_ref, twi_ref, o_ref, gs_scr, gh_scr):
    @pl.when(pl.program_id(1) == 0)
    def _():
        f2r = f2r_ref[...]
        f2i = f2i_ref[...]
        for j in range(MID_KB):
            gr, gi = _stacked_g(f2r, f2i, twr_ref[j:j + 1, :], twi_ref[j:j + 1, :])
            gs_scr[j] = _stack(gr, -gi, gi, gr).astype(BF16)
            grt, git = gr.T, gi.T
            gh_scr[j] = _stack(grt, git, -git, grt).astype(BF16)

    c = a_ref.shape[-1]
    for j in range(MID_KB):
        for cb in range(c // MID_COLS):
            cols = slice(cb * MID_COLS, (cb + 1) * MID_COLS)
            a = jnp.concatenate([a_ref[0, j, :, cols], a_ref[1, j, :, cols]], axis=0)
            x = _dot(gs_scr[j], a)
            xr, xi = x[:DFT_N2], x[DFT_N2:]
            kr, ki = kf_ref[0, j, :, cols].astype(F32), kf_ref[1, j, :, cols].astype(F32)
            y = jnp.concatenate([xr * kr - xi * ki, xr * ki + xi * kr], axis=0).astype(BF16)
            bp = _dot(gh_scr[j], y)
            o_ref[0, j, :, cols] = bp[:DFT_N2].astype(o_ref.dtype)
            o_ref[1, j, :, cols] = bp[DFT_N2:].astype(o_ref.dtype)


def _mid_filter_kernel(a_ref, f2r_ref, f2i_ref, twr_ref, twi_ref, o_ref):
    f2r = f2r_ref[...]
    f2i = f2i_ref[...]
    for j in range(MID_KB):
        gr, gi = _stacked_g(f2r, f2i, twr_ref[j:j + 1, :], twi_ref[j:j + 1, :])
        gs = _stack(gr, -gi, gi, gr).astype(BF16)
        a = jnp.concatenate([a_ref[0, j], a_ref[1, j]], axis=0)
        x = _dot(gs, a)
        o_ref[0, j] = x[:DFT_N2].astype(o_ref.dtype)
        o_ref[1, j] = x[DFT_N2:].astype(o_ref.dtype)


def _mid_specs(n1):
    sq = pl.BlockSpec((DFT_N2, DFT_N2), lambda k, p: (0, 0))
    tw = pl.BlockSpec((MID_KB, DFT_N2), lambda k, p: (k, 0))
    return sq, tw


def _mid_data(a5, kf, f2r, f2i, twr, twi):
    p, _, n1, n2, c = a5.shape
    sq, tw = _mid_specs(n1)
    blk = pl.BlockSpec((None, 2, MID_KB, n2, c), lambda k, pi: (pi, 0, k, 0, 0))
    return pl.pallas_call(
        _mid_data_kernel,
        out_shape=jax.ShapeDtypeStruct(a5.shape, BF16),
        grid=(n1 // MID_KB, p),
        in_specs=[blk, pl.BlockSpec((2, MID_KB, n2, c), lambda k, pi: (0, k, 0, 0)), sq, sq, tw, tw],
        out_specs=blk,
        scratch_shapes=[pltpu.VMEM((MID_KB, 2 * n2, 2 * n2), BF16), pltpu.VMEM((MID_KB, 2 * n2, 2 * n2), BF16)],
        compiler_params=pltpu.CompilerParams(dimension_semantics=("arbitrary", "arbitrary")),
        name="dft_mid",
    )(a5, kf, f2r, f2i, twr, twi)


def _mid_filter(a4, f2r, f2i, twr, twi):
    _, n1, n2, c = a4.shape
    sq, tw = _mid_specs(n1)
    blk = pl.BlockSpec((2, MID_KB, n2, c), lambda k, pi: (0, k, 0, 0))
    return pl.pallas_call(
        _mid_filter_kernel,
        out_shape=jax.ShapeDtypeStruct(a4.shape, BF16),
        grid=(n1 // MID_KB, 1),
        in_specs=[blk, sq, sq, tw, tw],
        out_specs=blk,
        compiler_params=pltpu.CompilerParams(dimension_semantics=("arbitrary", "arbitrary")),
        name="dft_mid_filter",
    )(a4, f2r, f2i, twr, twi)


def _attn_kernel(q_ref, k_ref, v_ref, bias_ref, o_ref, kt_scr, s_scr, m_scr, *, rows):
    rb = pl.program_id(1)
    lane = lax.broadcasted_iota(jnp.int32, (GRID_W, 128), 1)
    first = lane < HEAD_DIM
    band = WIN_ROWS * GRID_W
    ones = jnp.ones((band, 128), BF16)
    n_keys = rows * GRID_W
    units_per_block = ATT_ROWS * GRID_W // KT_UNIT
    n_trips = ATT_ROWS // ATT_UNROLL
    assert units_per_block == n_trips
    pairs = range(N_HEADS // 2)

    def transpose_unit(m):
        lanes = pl.ds(pl.multiple_of(m * KT_UNIT, KT_UNIT), KT_UNIT)
        kt_scr[0, :, lanes] = k_ref[pl.ds(pl.multiple_of(m * KT_UNIT, KT_UNIT), KT_UNIT), :].T
        src = jnp.minimum(m * KT_UNIT + GRID_W, n_keys - KT_UNIT)
        kt_scr[1, :, lanes] = k_ref[pl.ds(pl.multiple_of(src, GRID_W), KT_UNIT), :].T

    @pl.when(rb == 0)
    def _():
        for j in range(units_per_block):
            transpose_unit(j)

    nxt = jnp.minimum(rb + 1, pl.num_programs(1) - 1)

    def row_info(t):
        info = []
        for a in range(ATT_UNROLL):
            rl = t * ATT_UNROLL + a
            r = rb * ATT_ROWS + rl
            r0 = jnp.clip(r - WIN_ROWS // 2, 0, rows - WIN_ROWS)
            par = r0 % 2
            info.append((r - r0, par, (r0 - par) * GRID_W, r0 * GRID_W, pl.multiple_of(rl * GRID_W, GRID_W)))
        return info

    def scores(t, slot):
        for a, (d, par, kt_start, start, qoff) in enumerate(row_info(t)):
            for pr in pairs:
                cs = slice(pr * 128, (pr + 1) * 128)
                q2 = q_ref[pl.ds(qoff, GRID_W), cs]
                zero = jnp.zeros_like(q2)
                qbd = jnp.concatenate([jnp.where(first, q2, zero), jnp.where(first, zero, q2)], axis=0)
                bias = [bias_ref[2 * j + (WIN_ROWS - 1) - d, pr] for j in range(band // KT_UNIT)]
                kl = pl.ds(pl.multiple_of(kt_start, KT_UNIT), band)
                s = _dot(qbd, kt_scr[par, cs, kl]) + jnp.concatenate(bias, axis=1)
                s_scr[slot, a, pr] = s
                m_scr[slot, a, pr] = jnp.broadcast_to(jnp.max(s, axis=-1, keepdims=True), (2 * GRID_W, 128))

    def outputs(t, slot):
        for a, (d, par, kt_start, start, qoff) in enumerate(row_info(t)):
            for pr in pairs:
                cs = slice(pr * 128, (pr + 1) * 128)
                m = m_scr[slot, a, pr]
                p = jnp.exp2(s_scr[slot, a, pr] - jnp.concatenate([m] * (band // 128), axis=1)).astype(BF16)
                ks = pl.ds(pl.multiple_of(start, GRID_W), band)
                acc = _dot(p, jnp.concatenate([v_ref[ks, cs], ones], axis=1))
                o = acc[:, :128] / acc[:, 128:]
                o_ref[pl.ds(qoff, GRID_W), cs] = jnp.where(first, o[:GRID_W], o[GRID_W:]).astype(o_ref.dtype)

    scores(0, 0)
    for t in range(n_trips):
        transpose_unit(nxt * units_per_block + t)
        if t + 1 < n_trips:
            scores(t + 1, (t + 1) % 2)
        outputs(t, t % 2)


def _attention(z3, bias_tab):
    b, l, _ = z3.shape
    rows = l // GRID_W
    tq = ATT_ROWS * GRID_W
    return pl.pallas_call(
        functools.partial(_attn_kernel, rows=rows),
        out_shape=jax.ShapeDtypeStruct((b, l, ATT_W), BF16),
        grid=(b, rows // ATT_ROWS),
        in_specs=[
            pl.BlockSpec((None, tq, CB), lambda bi, i: (bi, i, Z_Q)),
            pl.BlockSpec((None, l, CB), lambda bi, i: (bi, 0, Z_K), pipeline_mode=pl.Buffered(1)),
            pl.BlockSpec((None, l, CB), lambda bi, i: (bi, 0, Z_V)),
            pl.BlockSpec(bias_tab.shape, lambda bi, i: (0, 0, 0, 0), pipeline_mode=pl.Buffered(1)),
        ],
        out_specs=pl.BlockSpec((None, tq, ATT_W), lambda bi, i: (bi, i, 0)),
        scratch_shapes=[pltpu.VMEM((2, ATT_W, l), BF16),
                        pltpu.VMEM((2, ATT_UNROLL, N_HEADS // 2, 2 * GRID_W, WIN_ROWS * GRID_W), F32),
                        pltpu.VMEM((2, ATT_UNROLL, N_HEADS // 2, 2 * GRID_W, 128), F32)],
        compiler_params=pltpu.CompilerParams(dimension_semantics=("arbitrary", "arbitrary")),
        name="nbr_attention",
    )(z3, z3, z3, bias_tab)


def _outproj_kernel(winv_ref, bp_ref, x_ref, u_ref, x0_ref, gh_ref, at_ref, ga_ref, hb_ref, onh_ref, ona_ref,
                    w_ref, o_ref, wk_scr):
    @pl.when((pl.program_id(0) == 0) & (pl.program_id(1) == 0))
    def _():
        _expand_kron(winv_ref, wk_scr)

    n1, rows, _ = x_ref.shape
    k = bp_ref.shape[0]
    tok = n1 * SUBLANES
    load = lambda ref: ref[...].astype(F32)
    bp, u, x0, gh, at, ga = (load(r) for r in (bp_ref, u_ref, x0_ref, gh_ref, at_ref, ga_ref))
    for h in range(rows // SUBLANES):
        sl = slice(h * SUBLANES, (h + 1) * SUBLANES)
        part = lambda v: v[:, sl, :].reshape(tok, v.shape[-1])
        ylong = _dot(wk_scr[...], bp[:, sl, :].reshape(k * SUBLANES, HY_W).astype(BF16))
        yh = part(x0) * (ylong + part(u) * hb_ref[...])
        yh = yh * lax.rsqrt(jnp.mean(yh * yh, axis=-1, keepdims=True) + EPS) * onh_ref[...]
        yh = yh * jax.nn.silu(part(gh))
        ya = part(at)
        ya = ya * lax.rsqrt(jnp.mean(ya * ya, axis=-1, keepdims=True) + EPS) * ona_ref[...]
        ya = ya * jax.nn.silu(part(ga))
        acc = _dot(yh.astype(BF16), w_ref[:HY_W, :]) + _dot(ya.astype(BF16), w_ref[HY_W:, :])
        o_ref[:, sl, :] = (x_ref[:, sl, :].reshape(tok, D_MODEL) + acc).reshape(n1, SUBLANES, D_MODEL)


def _outproj(w_inv, bp, x, u, x0c, z, att, hy_bias, on_hy, on_att, w_out_bf):
    p, n1, n2, _ = x.shape
    rows = min(n2, SLAB_ROWS * max(1, OUT_SLABS // n1))
    blk = lambda width, c: pl.BlockSpec((None, n1, rows, width), lambda pi, j, c=c: (pi, 0, j, c))
    vec = pl.BlockSpec((1, CB), lambda pi, j: (0, 0))
    return pl.pallas_call(
        _outproj_kernel,
        out_shape=jax.ShapeDtypeStruct(x.shape, F32),
        grid=(p, n2 // rows),
        in_specs=[pl.BlockSpec(w_inv.shape, lambda pi, j: (0, 0)),
                  pl.BlockSpec((None, 2 * n1, rows, CB), lambda pi, j: (pi, 0, j, 0)),
                  blk(D_MODEL, 0), blk(CB, 0), blk(CB, 0), blk(CB, Z_GATE_HY), blk(CB, 0), blk(CB, Z_GATE_ATT),
                  vec, vec, vec, pl.BlockSpec((HY_W + ATT_W, D_MODEL), lambda pi, j: (0, 0))],
        out_specs=blk(D_MODEL, 0),
        scratch_shapes=[pltpu.VMEM((n1 * SUBLANES, 2 * n1 * SUBLANES), BF16)],
        compiler_params=pltpu.CompilerParams(dimension_semantics=("arbitrary", "arbitrary")),
        name="outproj",
    )(w_inv, bp, x, u, x0c, z, att, z, hy_bias, on_hy, on_att, w_out_bf)


def _dft_tables(l):
    n = 2 * l
    n2 = DFT_N2
    n1 = n // n2
    h = n1 // 2
    k1 = np.arange(n1)
    ang1 = -2.0 * np.pi * ((k1[:, None] * k1[None, :]) % n1) / n1
    f1r, f1i = np.cos(ang1), np.sin(ang1)
    w_data = np.block([[f1r[:, :h], -f1i[:, :h]], [f1i[:, :h], f1r[:, :h]]])
    w_filt = np.concatenate([f1r, f1i], axis=0)
    w_inv = np.block([[f1r[:h], f1i[:h]], [-f1i[:h], f1r[:h]]]) / n
    k2 = np.arange(n2)
    ang2 = -2.0 * np.pi * ((k2[:, None] * k2[None, :]) % n2) / n2
    angt = -2.0 * np.pi * (k1[:, None] * k2[None, :]) / n
    f = lambda a: jnp.asarray(a, dtype=F32)
    return dict(n1=n1, w_data=f(w_data), w_filt=f(w_filt), w_inv=f(w_inv),
                f2r=f(np.cos(ang2)), f2i=f(np.sin(ang2)), twr=f(np.cos(angt)), twi=f(np.sin(angt)))


def _positional_features(l):
    t01 = jnp.linspace(0.0, 1.0, l, dtype=F32)[None, :]
    w = 2.0 * math.pi * jnp.arange(l, dtype=F32)[None, :] / l
    f = jnp.linspace(1e-4, POS_BANDS - 1, POS_BANDS, dtype=F32)[:, None]
    zt = jnp.concatenate([t01, jnp.cos(f * w), -jnp.sin(f * w)], axis=0)
    return jnp.pad(zt, ((0, 128 - zt.shape[0]), (0, 0))), t01.reshape(l, 1)


def _abs_deltas():
    min_decay = math.log(1e-2) / 1.5
    max_decay = math.log(1e-2) / 0.3
    return jnp.abs(jnp.linspace(min_decay, max_decay, 2 * HY_W, dtype=F32))[None, :]


def _bias_table(rpb):
    cols = np.arange(GRID_W)
    cstart = np.clip(cols - WIN_COLS // 2, 0, GRID_W - WIN_COLS)
    kc = np.arange(GRID_W)
    valid = (kc[None, :] >= cstart[:, None]) & (kc[None, :] < cstart[:, None] + WIN_COLS)
    rel = kc[None, :] - cols[:, None] + (WIN_COLS - 1)
    n_rel = 2 * WIN_COLS - 1
    ci, ki = np.nonzero(valid)
    sel = np.zeros((2, n_rel, GRID_W, 2, GRID_W), np.float32)
    for r in range(2):
        sel[r, rel[ci, ki], ci, r, ki] = 1.0
    outside = np.broadcast_to(np.where(valid, 0.0, NEG_INF)[:, None, :], (GRID_W, 2, GRID_W)).astype(np.float32)
    n_off = 2 * WIN_ROWS - 2
    pairs = jnp.stack([rpb[:, :-1], rpb[:, 1:]], axis=2).astype(F32)
    lhs = pairs.transpose(1, 0, 2, 3).reshape(n_off * N_HEADS, 2 * n_rel) * LOG2E
    t = jnp.dot(lhs, jnp.asarray(sel.reshape(2 * n_rel, -1)), precision=lax.Precision.HIGHEST)
    t = t + jnp.asarray(outside.reshape(1, -1))
    return t.reshape(n_off, N_HEADS // 2, 2 * GRID_W, 2 * GRID_W)


def _filter_spectrum(tabs, zpos, absd, f_w1, f_b1, f_fr1, f_w2, f_b2, f_fr2, f_w3):
    n1 = tabs["n1"]
    w1p = jnp.pad(f_w1, ((0, 128 - f_w1.shape[0]), (0, 0)))
    halves = _filter(zpos, w1p, f_b1[None], f_fr1[None], f_w2, f_b2[None], f_fr2[None], f_w3, absd)
    a = _slab_stage(tabs["w_filt"], [h.reshape(1, n1 // 2, DFT_N2, HY_W) for h in halves], BF16)
    return _mid_filter(a.reshape(2, n1, DFT_N2, HY_W), tabs["f2r"], tabs["f2i"], tabs["twr"], tabs["twi"])


def _layer(x, tabs, kf, bias_tab, bd, norm_g, w_in, conv_w, conv_b, hy_bias, qn_g, kn_g, on_hy, on_att, w_out):
    b, l, _ = x.shape
    t = b * l
    n1 = tabs["n1"]
    x2 = x.reshape(t, D_MODEL)
    qgain = jnp.tile(qn_g, N_HEADS)[None] * (HEAD_DIM ** -0.5 * LOG2E)
    kgain = jnp.tile(kn_g, N_HEADS)[None]
    z2, u, x0c = _inproj(x2, l, norm_g[None], w_in.astype(BF16), qgain, kgain, bd, conv_w, conv_b[None])
    z3 = z2.reshape(b, l, z2.shape[1])
    a = _slab_stage(tabs["w_data"], [u.reshape(b // 2, n1, DFT_N2, HY_W)], BF16)
    bp = _mid_data(a.reshape(b // 2, 2, n1, DFT_N2, HY_W), kf, tabs["f2r"], tabs["f2i"], tabs["twr"], tabs["twi"])
    att = _attention(z3, bias_tab)
    view = lambda v: v.reshape(b // 2, n1, DFT_N2, v.shape[-1])
    out = _outproj(tabs["w_inv"], bp.reshape(b // 2, 2 * n1, DFT_N2, HY_W), view(x), view(u), view(x0c), view(z2),
                   view(att), hy_bias[None], on_hy[None], on_att[None], w_out.astype(BF16))
    return out.reshape(b, l, D_MODEL)


def _trunk(x, norm_g, w_in, conv_w, conv_b, f_w1, f_b1, f_fr1, f_w2, f_b2, f_fr2, f_w3,
           hy_bias, qn_g, kn_g, rpb, on_hy, on_att, w_out):
    b, l, _ = x.shape
    assert b % 2 == 0 and l % (GRID_W * ATT_ROWS) == 0 and l // GRID_W >= WIN_ROWS
    assert (2 * l) % (DFT_N2 * MID_KB) == 0 and l % TOKEN_TILE == 0
    tabs = _dft_tables(l)
    zpos = _positional_features(l)
    absd = _abs_deltas()
    head = np.arange(CB) // HEAD_DIM
    bd = jnp.asarray((head[:, None] == head[None, :]).astype(np.float32) / HEAD_DIM, dtype=BF16)
    for i in range(norm_g.shape[0]):
        kf = _filter_spectrum(tabs, zpos, absd, f_w1[i], f_b1[i], f_fr1[i], f_w2[i], f_b2[i], f_fr2[i], f_w3[i])
        x = _layer(x, tabs, kf, _bias_table(rpb[i]), bd, norm_g[i], w_in[i], conv_w[i], conv_b[i], hy_bias[i],
                   qn_g[i], kn_g[i], on_hy[i], on_att[i], w_out[i])
    return x


def kernel(x_prompt, x_sample, norm_g, w_in, conv_w, conv_b, f_w1, f_b1, f_fr1, f_w2, f_b2, f_fr2, f_w3,
           hy_bias, qn_g, kn_g, rpb, on_hy, on_att, w_out):
    params = (norm_g, w_in, conv_w, conv_b, f_w1, f_b1, f_fr1, f_w2, f_b2, f_fr2, f_w3,
              hy_bias, qn_g, kn_g, rpb, on_hy, on_att, w_out)
    return (_trunk(x_prompt, *params), _trunk(x_sample, *params))
```
